```python
import jax, jax.numpy as jnp
from jax import lax
import numpy as np

D_MODEL = 1024
BATCH = 8
SEQ = 4096
DEPTH = 2
DEC_BATCH = 2
DEC_SEQ = 16384
PAST_LEN = 128

HEAD_DIM = 64
N_HEADS_CONV = 8
N_HEADS_SG = 8
D_CONV = N_HEADS_CONV * HEAD_DIM
D_SG = N_HEADS_SG * HEAD_DIM
D_MIX = D_CONV + D_SG
D_IN = 3 * D_CONV + 2 * D_SG
CONV_WIDTH = 3
CHUNK = 128
D_FF = 2816
N_MOD = 9
EPS = 1e-6

kernel_name = "hybrid_conv_sgu_macaron_adaln_encoder"


def rms_norm(x, g):
    xf = x.astype(jnp.float32)
    y = xf * lax.rsqrt(jnp.mean(xf * xf, axis=-1, keepdims=True) + EPS)
    return (y * g.astype(jnp.float32)).astype(x.dtype)


def layer_norm(x, g):
    xf = x.astype(jnp.float32)
    mu = jnp.mean(xf, axis=-1, keepdims=True)
    d = xf - mu
    y = d * lax.rsqrt(jnp.mean(d * d, axis=-1, keepdims=True) + EPS)
    return (y * g.astype(jnp.float32)).astype(x.dtype)


def swiglu(h, w1, w2):
    gate, up = jnp.split(h @ w1, 2, axis=-1)
    return (jax.nn.silu(gate) * up) @ w2


def centred_depthwise_conv3(h, w):
    hp = jnp.pad(h, ((0, 0), (1, 1), (0, 0)))
    return hp[:, :-2] * w[0] + hp[:, 1:-1] * w[1] + hp[:, 2:] * w[2]


def token_mixing(h, w_in, conv_w, sg_norm_g, sg_ws, sg_bs, grp_norm_g, w_out):
    bsz, seq, _ = h.shape
    z = h @ w_in
    b_gate = z[..., :D_CONV]
    c_gate = z[..., D_CONV:2 * D_CONV]
    x_in = z[..., 2 * D_CONV:3 * D_CONV]
    uv = jax.nn.gelu(z[..., 3 * D_CONV:], approximate=False)
    y_a = b_gate * centred_depthwise_conv3(c_gate * x_in, conv_w)
    u, v = uv[..., :D_SG], uv[..., D_SG:]
    v = layer_norm(v, sg_norm_g)
    vc = v.reshape(bsz, seq // CHUNK, CHUNK, N_HEADS_SG, HEAD_DIM)
    sv = jnp.einsum('hpq,bnqhd->bnphd', sg_ws, vc) + sg_bs.T[None, None, :, :, None]
    y_b = u * sv.reshape(bsz, seq, D_SG)
    y = jnp.concatenate([rms_norm(y_a, grp_norm_g[:D_CONV]),
                         rms_norm(y_b, grp_norm_g[D_CONV:])], axis=-1)
    return y @ w_out


def trunk(x, c, ada_w, ada_b, norm_g, ffn_w1, ffn_w2, mix_w_in, conv_w,
          sg_norm_g, sg_ws, sg_bs, grp_norm_g, mix_w_out, final_g):
    bsz = x.shape[0]
    sc = jax.nn.silu(c)
    for l in range(DEPTH):
        mod = (sc @ ada_w[l] + ada_b[l]).reshape(bsz, N_MOD, 1, D_MODEL)

        def modulate(t, k):
            return rms_norm(t, norm_g[l, k]) * (1 + mod[:, 3 * k + 1]) + mod[:, 3 * k]

        x = x + 0.5 * mod[:, 2] * swiglu(modulate(x, 0), ffn_w1[l, 0], ffn_w2[l, 0])
        x = x + mod[:, 5] * token_mixing(modulate(x, 1), mix_w_in[l], conv_w[l], sg_norm_g[l],
                                         sg_ws[l], sg_bs[l], grp_norm_g[l], mix_w_out[l])
        x = x + 0.5 * mod[:, 8] * swiglu(modulate(x, 2), ffn_w1[l, 1], ffn_w2[l, 1])
    return rms_norm(x, final_g)


def setup_inputs(seed: int = 0) -> dict:
    key = jax.random.key(seed)
    ks = jax.random.split(key, 20)
    f32 = jnp.float32
    nrm = lambda k, s, sc: jax.random.normal(k, s, f32) * sc
    return {
        "x_prompt": nrm(ks[0], (BATCH, SEQ, D_MODEL), 1.0),
        "x_sample": nrm(ks[1], (DEC_BATCH, DEC_SEQ, D_MODEL), 1.0),
        "c_prompt": nrm(ks[2], (BATCH, D_MODEL), 1.0),
        "c_sample": nrm(ks[3], (DEC_BATCH, D_MODEL), 1.0),
        "ada_w": nrm(ks[4], (DEPTH, D_MODEL, N_MOD * D_MODEL), 0.5 * D_MODEL ** -0.5),
        "ada_b": nrm(ks[5], (DEPTH, N_MOD * D_MODEL), 0.01),
        "norm_g": 1.0 + nrm(ks[6], (DEPTH, 3, D_MODEL), 0.02),
        "ffn_w1": nrm(ks[7], (DEPTH, 2, D_MODEL, 2 * D_FF), D_MODEL ** -0.5),
        "ffn_w2": nrm(ks[8], (DEPTH, 2, D_FF, D_MODEL), D_FF ** -0.5),
        "mix_w_in": nrm(ks[9], (DEPTH, D_MODEL, D_IN), D_MODEL ** -0.5),
        "conv_w": nrm(ks[10], (DEPTH, CONV_WIDTH, D_CONV), CONV_WIDTH ** -0.5),
        "sg_norm_g": 1.0 + nrm(ks[11], (DEPTH, D_SG), 0.02),
        "sg_ws": nrm(ks[12], (DEPTH, N_HEADS_SG, CHUNK, CHUNK), CHUNK ** -0.5),
        "sg_bs": 1.0 + nrm(ks[13], (DEPTH, N_HEADS_SG, CHUNK), 0.01),
        "grp_norm_g": 1.0 + nrm(ks[14], (DEPTH, D_MIX), 0.02),
        "mix_w_out": nrm(ks[15], (DEPTH, D_MIX, D_MODEL), D_MIX ** -0.5),
        "final_g": 1.0 + nrm(ks[16], (D_MODEL,), 0.02),
    }


def reference(x_prompt, x_sample, c_prompt, c_sample, ada_w, ada_b, norm_g, ffn_w1, ffn_w2,
              mix_w_in, conv_w, sg_norm_g, sg_ws, sg_bs, grp_norm_g, mix_w_out, final_g):
    y_prompt = trunk(x_prompt, c_prompt, ada_w, ada_b, norm_g, ffn_w1, ffn_w2, mix_w_in, conv_w,
                     sg_norm_g, sg_ws, sg_bs, grp_norm_g, mix_w_out, final_g)
    y_sample = trunk(x_sample, c_sample, ada_w, ada_b, norm_g, ffn_w1, ffn_w2, mix_w_in, conv_w,
                     sg_norm_g, sg_ws, sg_bs, grp_norm_g, mix_w_out, final_g)
    return (y_prompt, y_sample)
```

```python
import functools

import numpy as np
import jax
import jax.numpy as jnp
from jax import lax
from jax.experimental import pallas as pl
from jax.experimental.pallas import tpu as pltpu

D_MODEL = 1024
DEPTH = 2
HEAD_DIM = 64
N_HEADS_SG = 8
D_CONV = 512
D_SG = 512
D_MIX = D_CONV + D_SG
D_IN = 3 * D_CONV + 2 * D_SG
CHUNK = 128
D_FF = 2816
N_MOD = 9
EPS = 1e-6

BF16 = jnp.bfloat16
F32 = jnp.float32

MXU_COLS_V7X = 256
SUBLANES_F32 = 8
VMEM_LIMIT_BYTES = 56 * 1024 * 1024

TOKEN_TILE = 512
HALO_ROWS = 2 * SUBLANES_F32
FF_COLS = MXU_COLS_V7X
MOD_ROWS = 16
ADA_COLS = 1024
HEADS_PER_GROUP = 4
GROUP_COLS = HEADS_PER_GROUP * HEAD_DIM


def _resident(block_shape, index_map):
  return pl.BlockSpec(block_shape, index_map, pipeline_mode=pl.Buffered(1))


def _mod_norm(x, g, scale, shift):
  ms = jnp.mean(x * x, axis=-1, keepdims=True)
  y = (x * lax.rsqrt(ms + EPS)) * g
  return y * (1.0 + scale) + shift


def _rms(x, g):
  ms = jnp.mean(x * x, axis=-1, keepdims=True)
  return (x * lax.rsqrt(ms + EPS)) * g


def _adaln_kernel(c_ref, w_ref, b_ref, o_ref):
  c = c_ref[...]
  sc = (c * jax.nn.sigmoid(c)).astype(BF16)
  w = w_ref[...].astype(BF16)
  o_ref[...] = jnp.dot(sc, w, preferred_element_type=F32) + b_ref[...]


def _adaln(c_all, ada_w, ada_b):
  n_out = N_MOD * D_MODEL
  return pl.pallas_call(
      _adaln_kernel,
      grid=(DEPTH, n_out // ADA_COLS),
      in_specs=[
          pl.BlockSpec((MOD_ROWS, D_MODEL), lambda l, j: (0, 0)),
          pl.BlockSpec((None, D_MODEL, ADA_COLS), lambda l, j: (l, 0, j)),
          pl.BlockSpec((None, 1, ADA_COLS), lambda l, j: (l, 0, j)),
      ],
      out_specs=pl.BlockSpec((None, MOD_ROWS, ADA_COLS), lambda l, j: (l, 0, j)),
      out_shape=jax.ShapeDtypeStruct((DEPTH, MOD_ROWS, n_out), F32),
      compiler_params=pltpu.CompilerParams(
          dimension_semantics=("arbitrary", "arbitrary"),
          vmem_limit_bytes=VMEM_LIMIT_BYTES),
      name="adaln_mod",
  )(c_all, ada_w, ada_b.reshape(DEPTH, 1, n_out))


def _ffn_kernel(k, final, x_ref, mod_ref, ng_ref, w1_ref, w2_ref, fg_ref, o_ref,
                h_scr, a_scr):
  shift = mod_ref[3 * k:3 * k + 1, :]
  scale = mod_ref[3 * k + 1:3 * k + 2, :]
  gate = mod_ref[3 * k + 2:3 * k + 3, :]
  h_scr[...] = _mod_norm(x_ref[...], ng_ref[k:k + 1, :], scale, shift).astype(BF16)
  for c in range(D_FF // FF_COLS):
    lo = c * FF_COLS
    gt = jnp.dot(h_scr[...], w1_ref[:, lo:lo + FF_COLS], preferred_element_type=F32)
    up = jnp.dot(h_scr[...], w1_ref[:, D_FF + lo:D_FF + lo + FF_COLS],
                 preferred_element_type=F32)
    a_scr[:, lo:lo + FF_COLS] = ((gt * jax.nn.sigmoid(gt)) * up).astype(BF16)
  out = jnp.dot(a_scr[...], w2_ref[...], preferred_element_type=F32)
  y = x_ref[...] + (0.5 * gate) * out
  if final:
    y = _rms(y, fg_ref[...])
  o_ref[...] = y


def _ffn(x, mod, norm_g, w1, w2, final_g, *, layer, which, row0, final):
  bsz, seq, _ = x.shape
  k = 2 * which
  t = TOKEN_TILE
  return pl.pallas_call(
      functools.partial(_ffn_kernel, k, final),
      grid=(bsz, seq // t),
      in_specs=[
          pl.BlockSpec((None, t, D_MODEL), lambda b, i: (b, i, 0)),
          pl.BlockSpec((None, None, N_MOD, D_MODEL), lambda b, i: (layer, row0 + b, 0, 0)),
          _resident((None, 3, D_MODEL), lambda b, i: (layer, 0, 0)),
          _resident((None, None, D_MODEL, 2 * D_FF), lambda b, i: (layer, which, 0, 0)),
          _resident((None, None, D_FF, D_MODEL), lambda b, i: (layer, which, 0, 0)),
          _resident((1, D_MODEL), lambda b, i: (0, 0)),
      ],
      out_specs=pl.BlockSpec((None, t, D_MODEL), lambda b, i: (b, i, 0)),
      out_shape=jax.ShapeDtypeStruct(x.shape, F32),
      scratch_shapes=[pltpu.VMEM((t, D_MODEL), BF16), pltpu.VMEM((t, D_FF), BF16)],
      compiler_params=pltpu.CompilerParams(
          dimension_semantics=("arbitrary", "arbitrary"),
          vmem_limit_bytes=VMEM_LIMIT_BYTES),
      name=f"ffn_l{layer}_{which}",
  )(x, mod, norm_g, w1, w2, final_g)


def _mixer_kernel(x_ref, xp_ref, xn_ref, mod_ref, ng_ref, win_ref, cw_ref, sgg_ref,
                  wcat_ref, sbias_ref, gg_ref, wout_ref, o_ref, h_scr, y_scr):
  t = TOKEN_TILE
  i = pl.program_id(1)
  n_i = pl.num_programs(1)
  shift = mod_ref[3:4, :]
  scale = mod_ref[4:5, :]
  gate = mod_ref[5:6, :]
  g = ng_ref[1:2, :]

  h_scr[0:t, :] = _mod_norm(x_ref[...], g, scale, shift).astype(BF16)
  x_halo = jnp.concatenate([xp_ref[...], xn_ref[...]], axis=0)
  h_scr[t:t + HALO_ROWS, :] = _mod_norm(x_halo, g, scale, shift).astype(BF16)

  zc = jnp.dot(h_scr[...], win_ref[:, D_CONV:2 * D_CONV], preferred_element_type=F32)
  zx = jnp.dot(h_scr[...], win_ref[:, 2 * D_CONV:3 * D_CONV], preferred_element_type=F32)
  p_ext = zc * zx
  p = p_ext[0:t, :]
  p_prev = jnp.where(i > 0, p_ext[t + SUBLANES_F32 - 1:t + SUBLANES_F32, :], 0.0)
  p_next = jnp.where(i < n_i - 1, p_ext[t + SUBLANES_F32:t + SUBLANES_F32 + 1, :], 0.0)
  row = lax.broadcasted_iota(jnp.int32, (t, 1), 0)
  p_m1 = jnp.where(row == 0, p_prev, pltpu.roll(p, 1, axis=0))
  p_p1 = jnp.where(row == t - 1, p_next, pltpu.roll(p, t - 1, axis=0))
  conv = p_m1 * cw_ref[0:1, :] + p * cw_ref[1:2, :] + p_p1 * cw_ref[2:3, :]
  b_gate = jnp.dot(h_scr[0:t, :], win_ref[:, 0:D_CONV], preferred_element_type=F32)
  y_a = b_gate * conv
  y_scr[:, 0:D_CONV] = _rms(y_a, gg_ref[:, 0:D_CONV]).astype(BF16)

  sqrt_half = np.float32(np.sqrt(0.5))
  zu = jnp.dot(h_scr[0:t, :], win_ref[:, 3 * D_CONV:3 * D_CONV + D_SG],
               preferred_element_type=F32)
  zv = jnp.dot(h_scr[0:t, :], win_ref[:, 3 * D_CONV + D_SG:D_IN],
               preferred_element_type=F32)
  u = 0.5 * zu * (1.0 + lax.erf(zu * sqrt_half))
  v = 0.5 * zv * (1.0 + lax.erf(zv * sqrt_half))
  mu = jnp.mean(v, axis=-1, keepdims=True)
  dv = v - mu
  v = ((dv * lax.rsqrt(jnp.mean(dv * dv, axis=-1, keepdims=True) + EPS))
       * sgg_ref[...])
  lane_head = lax.broadcasted_iota(jnp.int32, (CHUNK, GROUP_COLS), 1) // HEAD_DIM
  sv_chunks = []
  for n in range(t // CHUNK):
    groups = []
    for gi in range(N_HEADS_SG // HEADS_PER_GROUP):
      vc = v[n * CHUNK:(n + 1) * CHUNK, gi * GROUP_COLS:(gi + 1) * GROUP_COLS]
      v_blk = jnp.concatenate(
          [jnp.where(lane_head == j, vc, 0.0).astype(BF16)
           for j in range(HEADS_PER_GROUP)], axis=0)
      w_g = wcat_ref[:, gi * HEADS_PER_GROUP * CHUNK:(gi + 1) * HEADS_PER_GROUP * CHUNK]
      groups.append(jnp.dot(w_g, v_blk, preferred_element_type=F32))
    sv_chunks.append(jnp.concatenate(groups, axis=1) + sbias_ref[...])
  sv = jnp.concatenate(sv_chunks, axis=0)
  y_b = u * sv
  y_scr[:, D_CONV:D_MIX] = _rms(y_b, gg_ref[:, D_CONV:D_MIX]).astype(BF16)

  out = jnp.dot(y_scr[...], wout_ref[...], preferred_element_type=F32)
  o_ref[...] = x_ref[...] + gate * out


def _mixer(x, mod, norm_g, w_in, conv_w, sg_norm_g, wcat, sbias, grp_g, w_out, *,
           layer, row0):
  bsz, seq, _ = x.shape
  t = TOKEN_TILE
  r = t // SUBLANES_F32
  n_blk = seq // SUBLANES_F32
  return pl.pallas_call(
      _mixer_kernel,
      grid=(bsz, seq // t),
      in_specs=[
          pl.BlockSpec((None, t, D_MODEL), lambda b, i: (b, i, 0)),
          pl.BlockSpec((None, SUBLANES_F32, D_MODEL),
                       lambda b, i: (b, jnp.maximum(i * r - 1, 0), 0)),
          pl.BlockSpec((None, SUBLANES_F32, D_MODEL),
                       lambda b, i: (b, jnp.minimum((i + 1) * r, n_blk - 1), 0)),
          pl.BlockSpec((None, None, N_MOD, D_MODEL), lambda b, i: (layer, row0 + b, 0, 0)),
          _resident((None, 3, D_MODEL), lambda b, i: (layer, 0, 0)),
          _resident((None, D_MODEL, D_IN), lambda b, i: (layer, 0, 0)),
          _resident((None, 3, D_CONV), lambda b, i: (layer, 0, 0)),
          _resident((None, 1, D_SG), lambda b, i: (layer, 0, 0)),
          _resident((None, CHUNK, N_HEADS_SG * CHUNK), lambda b, i: (layer, 0, 0)),
          _resident((None, CHUNK, D_SG), lambda b, i: (layer, 0, 0)),
          _resident((None, 1, D_MIX), lambda b, i: (layer, 0, 0)),
          _resident((None, D_MIX, D_MODEL), lambda b, i: (layer, 0, 0)),
      ],
      out_specs=pl.BlockSpec((None, t, D_MODEL), lambda b, i: (b, i, 0)),
      out_shape=jax.ShapeDtypeStruct(x.shape, F32),
      scratch_shapes=[pltpu.VMEM((t + HALO_ROWS, D_MODEL), BF16),
                      pltpu.VMEM((t, D_MIX), BF16)],
      compiler_params=pltpu.CompilerParams(
          dimension_semantics=("arbitrary", "arbitrary"),
          vmem_limit_bytes=VMEM_LIMIT_BYTES),
      name=f"mixer_l{layer}",
  )(x, x, x, mod, norm_g, w_in, conv_w, sg_norm_g, wcat, sbias, grp_g, w_out)


def kernel(x_prompt, x_sample, c_prompt, c_sample, ada_w, ada_b, norm_g, ffn_w1, ffn_w2,
           mix_w_in, conv_w, sg_norm_g, sg_ws, sg_bs, grp_norm_g, mix_w_out, final_g):
  n_prompt = c_prompt.shape[0]
  n_cond = n_prompt + c_sample.shape[0]
  c_all = jnp.concatenate(
      [c_prompt, c_sample, jnp.zeros((MOD_ROWS - n_cond, D_MODEL), F32)], axis=0)
  mod = _adaln(c_all, ada_w, ada_b).reshape(DEPTH, MOD_ROWS, N_MOD, D_MODEL)

  w1 = ffn_w1.astype(BF16)
  w2 = ffn_w2.astype(BF16)
  w_in = mix_w_in.astype(BF16)
  w_out = mix_w_out.astype(BF16)
  wcat = jnp.transpose(sg_ws, (0, 2, 1, 3)).reshape(
      DEPTH, CHUNK, N_HEADS_SG * CHUNK).astype(BF16)
  sbias = jnp.repeat(jnp.transpose(sg_bs, (0, 2, 1)), HEAD_DIM, axis=2)
  sgg = sg_norm_g.reshape(DEPTH, 1, D_SG)
  grp_g = grp_norm_g.reshape(DEPTH, 1, D_MIX)
  fg = final_g.reshape(1, D_MODEL)

  def trunk(x, row0):
    for l in range(DEPTH):
      x = _ffn(x, mod, norm_g, w1, w2, fg, layer=l, which=0, row0=row0, final=False)
      x = _mixer(x, mod, norm_g, w_in, conv_w, sgg, wcat, sbias, grp_g, w_out,
                 layer=l, row0=row0)
      x = _ffn(x, mod, norm_g, w1, w2, fg, layer=l, which=1, row0=row0,
               final=(l == DEPTH - 1))
    return x

  return trunk(x_prompt, 0), trunk(x_sample, n_prompt)
```

```python
import functools

import numpy as np
import jax
import jax.numpy as jnp
from jax import lax
from jax.experimental import pallas as pl
from jax.experimental.pallas import tpu as pltpu

D_MODEL = 1024
DEPTH = 2
HEAD_DIM = 64
N_HEADS_SG = 8
D_CONV = 512
D_SG = 512
D_MIX = D_CONV + D_SG
D_IN = 3 * D_CONV + 2 * D_SG
CHUNK = 128
D_FF = 2816
N_MOD = 9
EPS = 1e-6

BF16 = jnp.bfloat16
F32 = jnp.float32

MXU_COLS_V7X = 256
SUBLANES_F32 = 8
VMEM_LIMIT_BYTES = 56 * 1024 * 1024

TOKEN_TILE = 512
HALO_ROWS = 2 * SUBLANES_F32
FF_COLS = MXU_COLS_V7X
OUT_COLS = MXU_COLS_V7X
MOD_ROWS = 16
ADA_COLS = 1024
HEADS_PER_GROUP = 4
GROUP_COLS = HEADS_PER_GROUP * HEAD_DIM


def _resident(block_shape, index_map):
  return pl.BlockSpec(block_shape, index_map, pipeline_mode=pl.Buffered(1))


def _mod_norm(x, g, scale, shift):
  ms = jnp.mean(x * x, axis=-1, keepdims=True)
  y = (x * lax.rsqrt(ms + EPS)) * g
  return y * (1.0 + scale) + shift


def _rms(x, g):
  ms = jnp.mean(x * x, axis=-1, keepdims=True)
  return (x * lax.rsqrt(ms + EPS)) * g


def _adaln_kernel(c_ref, w_ref, b_ref, o_ref):
  c = c_ref[...]
  sc = (c * jax.nn.sigmoid(c)).astype(BF16)
  w = w_ref[...].astype(BF16)
  o_ref[...] = jnp.dot(sc, w, preferred_element_type=F32) + b_ref[...]


def _adaln(c_all, ada_w, ada_b):
  n_out = N_MOD * D_MODEL
  return pl.pallas_call(
      _adaln_kernel,
      grid=(DEPTH, n_out // ADA_COLS),
      in_specs=[
          pl.BlockSpec((MOD_ROWS, D_MODEL), lambda l, j: (0, 0)),
          pl.BlockSpec((None, D_MODEL, ADA_COLS), lambda l, j: (l, 0, j)),
          pl.BlockSpec((None, 1, ADA_COLS), lambda l, j: (l, 0, j)),
      ],
      out_specs=pl.BlockSpec((None, MOD_ROWS, ADA_COLS), lambda l, j: (l, 0, j)),
      out_shape=jax.ShapeDtypeStruct((DEPTH, MOD_ROWS, n_out), F32),
      compiler_params=pltpu.CompilerParams(
          dimension_semantics=("arbitrary", "arbitrary"),
          vmem_limit_bytes=VMEM_LIMIT_BYTES),
      name="adaln_mod",
  )(c_all, ada_w, ada_b.reshape(DEPTH, 1, n_out))


def _ffn_kernel(k, final, x_ref, xnext_ref, mod_ref, modnext_ref, ng_ref, w1_ref, w2_ref,
                fg_ref, o_ref, h_scr, a_scr):
  g = ng_ref[k:k + 1, :]

  def first_half(x_r, mod_r):
    x = x_r[...]
    inv = lax.rsqrt(jnp.mean(x * x, axis=-1, keepdims=True) + EPS)
    scale1 = 1.0 + mod_r[3 * k + 1:3 * k + 2, :]
    shift = mod_r[3 * k:3 * k + 1, :]
    for kb in range(D_MODEL // MXU_COLS_V7X):
      cs = slice(kb * MXU_COLS_V7X, (kb + 1) * MXU_COLS_V7X)
      h_scr[:, cs] = (((x_r[:, cs] * inv) * g[:, cs]) * scale1[:, cs]
                      + shift[:, cs]).astype(BF16)
    for c in range(D_FF // FF_COLS):
      lo = c * FF_COLS
      gt = jnp.dot(h_scr[...], w1_ref[:, lo:lo + FF_COLS], preferred_element_type=F32)
      up = jnp.dot(h_scr[...], w1_ref[:, D_FF + lo:D_FF + lo + FF_COLS],
                   preferred_element_type=F32)
      a_scr[:, lo:lo + FF_COLS] = ((gt * jax.nn.sigmoid(gt)) * up).astype(BF16)

  @pl.when(pl.program_id(0) == 0)
  def _():
    first_half(x_ref, mod_ref)

  out = jnp.dot(a_scr[...], w2_ref[...], preferred_element_type=F32)
  y = x_ref[...] + (0.5 * mod_ref[3 * k + 2:3 * k + 3, :]) * out
  if final:
    y = _rms(y, fg_ref[...])
  o_ref[...] = y
  first_half(xnext_ref, modnext_ref)


def _ffn(x, mod, norm_g, w1, w2, final_g, *, layer, which, row0, final):
  bsz, seq, _ = x.shape
  k = 2 * which
  t = TOKEN_TILE
  nt = seq // t
  n = bsz * nt
  nxt = lambda s: jnp.minimum(s + 1, n - 1)
  out = pl.pallas_call(
      functools.partial(_ffn_kernel, k, final),
      grid=(n,),
      in_specs=[
          pl.BlockSpec((t, D_MODEL), lambda s: (s, 0)),
          pl.BlockSpec((t, D_MODEL), lambda s: (nxt(s), 0)),
          pl.BlockSpec((None, None, N_MOD, D_MODEL), lambda s: (layer, row0 + s // nt, 0, 0)),
          pl.BlockSpec((None, None, N_MOD, D_MODEL),
                       lambda s: (layer, row0 + nxt(s) // nt, 0, 0)),
          _resident((None, 3, D_MODEL), lambda s: (layer, 0, 0)),
          _resident((None, None, D_MODEL, 2 * D_FF), lambda s: (layer, which, 0, 0)),
          _resident((None, None, D_FF, D_MODEL), lambda s: (layer, which, 0, 0)),
          _resident((1, D_MODEL), lambda s: (0, 0)),
      ],
      out_specs=pl.BlockSpec((t, D_MODEL), lambda s: (s, 0)),
      out_shape=jax.ShapeDtypeStruct((bsz * seq, D_MODEL), F32),
      scratch_shapes=[pltpu.VMEM((t, D_MODEL), BF16), pltpu.VMEM((t, D_FF), BF16)],
      compiler_params=pltpu.CompilerParams(
          dimension_semantics=("arbitrary",),
          vmem_limit_bytes=VMEM_LIMIT_BYTES),
      name=f"ffn_l{layer}_{which}",
  )(x.reshape(bsz * seq, D_MODEL), x.reshape(bsz * seq, D_MODEL), mod, mod, norm_g, w1, w2,
    final_g)
  return out.reshape(bsz, seq, D_MODEL)


def _mixer_kernel(x_ref, xp_ref, xn_ref, mod_ref, ng_ref, win_ref, cw_ref, sgg_ref,
                  wcat_ref, sbias_ref, gg_ref, wout_ref, o_ref, h_scr, y_scr):
  t = TOKEN_TILE
  i = pl.program_id(1)
  n_i = pl.num_programs(1)
  shift = mod_ref[3:4, :]
  scale = mod_ref[4:5, :]
  gate = mod_ref[5:6, :]
  g = ng_ref[1:2, :]

  h_scr[0:t, :] = _mod_norm(x_ref[...], g, scale, shift).astype(BF16)
  x_halo = jnp.concatenate([xp_ref[...], xn_ref[...]], axis=0)
  h_scr[t:t + HALO_ROWS, :] = _mod_norm(x_halo, g, scale, shift).astype(BF16)

  zc = jnp.dot(h_scr[...], win_ref[:, D_CONV:2 * D_CONV], preferred_element_type=F32)
  zx = jnp.dot(h_scr[...], win_ref[:, 2 * D_CONV:3 * D_CONV], preferred_element_type=F32)
  p_ext = zc * zx
  p = p_ext[0:t, :]
  p_prev = jnp.where(i > 0, p_ext[t + SUBLANES_F32 - 1:t + SUBLANES_F32, :], 0.0)
  p_next = jnp.where(i < n_i - 1, p_ext[t + SUBLANES_F32:t + SUBLANES_F32 + 1, :], 0.0)
  row = lax.broadcasted_iota(jnp.int32, (t, 1), 0)
  p_m1 = jnp.where(row == 0, p_prev, pltpu.roll(p, 1, axis=0))
  p_p1 = jnp.where(row == t - 1, p_next, pltpu.roll(p, t - 1, axis=0))
  conv = p_m1 * cw_ref[0:1, :] + p * cw_ref[1:2, :] + p_p1 * cw_ref[2:3, :]
  b_gate = jnp.dot(h_scr[0:t, :], win_ref[:, 0:D_CONV], preferred_element_type=F32)
  y_a = b_gate * conv
  y_scr[:, 0:D_CONV] = _rms(y_a, gg_ref[:, 0:D_CONV]).astype(BF16)

  sqrt_half = np.float32(np.sqrt(0.5))
  zu = jnp.dot(h_scr[0:t, :], win_ref[:, 3 * D_CONV:3 * D_CONV + D_SG],
               preferred_element_type=F32)
  zv = jnp.dot(h_scr[0:t, :], win_ref[:, 3 * D_CONV + D_SG:D_IN],
               preferred_element_type=F32)
  u = 0.5 * zu * (1.0 + lax.erf(zu * sqrt_half))
  v = 0.5 * zv * (1.0 + lax.erf(zv * sqrt_half))
  mu = jnp.mean(v, axis=-1, keepdims=True)
  dv = v - mu
  v = ((dv * lax.rsqrt(jnp.mean(dv * dv, axis=-1, keepdims=True) + EPS))
       * sgg_ref[...])
  lane_head = lax.broadcasted_iota(jnp.int32, (CHUNK, GROUP_COLS), 1) // HEAD_DIM
  sv_chunks = []
  for n in range(t // CHUNK):
    groups = []
    for gi in range(N_HEADS_SG // HEADS_PER_GROUP):
      vc = v[n * CHUNK:(n + 1) * CHUNK, gi * GROUP_COLS:(gi + 1) * GROUP_COLS]
      v_blk = jnp.concatenate(
          [jnp.where(lane_head == j, vc, 0.0).astype(BF16)
           for j in range(HEADS_PER_GROUP)], axis=0)
      w_g = wcat_ref[:, gi * HEADS_PER_GROUP * CHUNK:(gi + 1) * HEADS_PER_GROUP * CHUNK]
      groups.append(jnp.dot(w_g, v_blk, preferred_element_type=F32))
    sv_chunks.append(jnp.concatenate(groups, axis=1) + sbias_ref[...])
  sv = jnp.concatenate(sv_chunks, axis=0)
  y_b = u * sv
  y_scr[:, D_CONV:D_MIX] = _rms(y_b, gg_ref[:, D_CONV:D_MIX]).astype(BF16)

  out = jnp.dot(y_scr[...], wout_ref[...], preferred_element_type=F32)
  o_ref[...] = x_ref[...] + gate * out


def _mixer(x, mod, norm_g, w_in, conv_w, sg_norm_g, wcat, sbias, grp_g, w_out, *,
           layer, row0):
  bsz, seq, _ = x.shape
  t = TOKEN_TILE
  r = t // SUBLANES_F32
  n_blk = seq // SUBLANES_F32
  return pl.pallas_call(
      _mixer_kernel,
      grid=(bsz, seq // t),
      in_specs=[
          pl.BlockSpec((None, t, D_MODEL), lambda b, i: (b, i, 0)),
          pl.BlockSpec((None, SUBLANES_F32, D_MODEL),
                       lambda b, i: (b, jnp.maximum(i * r - 1, 0), 0)),
          pl.BlockSpec((None, SUBLANES_F32, D_MODEL),
                       lambda b, i: (b, jnp.minimum((i + 1) * r, n_blk - 1), 0)),
          pl.BlockSpec((None, None, N_MOD, D_MODEL), lambda b, i: (layer, row0 + b, 0, 0)),
          _resident((None, 3, D_MODEL), lambda b, i: (layer, 0, 0)),
          _resident((None, D_MODEL, D_IN), lambda b, i: (layer, 0, 0)),
          _resident((None, 3, D_CONV), lambda b, i: (layer, 0, 0)),
          _resident((None, 1, D_SG), lambda b, i: (layer, 0, 0)),
          _resident((None, CHUNK, N_HEADS_SG * CHUNK), lambda b, i: (layer, 0, 0)),
          _resident((None, CHUNK, D_SG), lambda b, i: (layer, 0, 0)),
          _resident((None, 1, D_MIX), lambda b, i: (layer, 0, 0)),
          _resident((None, D_MIX, D_MODEL), lambda b, i: (layer, 0, 0)),
      ],
      out_specs=pl.BlockSpec((None, t, D_MODEL), lambda b, i: (b, i, 0)),
      out_shape=jax.ShapeDtypeStruct(x.shape, F32),
      scratch_shapes=[pltpu.VMEM((t + HALO_ROWS, D_MODEL), BF16),
                      pltpu.VMEM((t, D_MIX), BF16)],
      compiler_params=pltpu.CompilerParams(
          dimension_semantics=("arbitrary", "arbitrary"),
          vmem_limit_bytes=VMEM_LIMIT_BYTES),
      name=f"mixer_l{layer}",
  )(x, x, x, mod, norm_g, w_in, conv_w, sg_norm_g, wcat, sbias, grp_g, w_out)


def kernel(x_prompt, x_sample, c_prompt, c_sample, ada_w, ada_b, norm_g, ffn_w1, ffn_w2,
           mix_w_in, conv_w, sg_norm_g, sg_ws, sg_bs, grp_norm_g, mix_w_out, final_g):
  n_prompt = c_prompt.shape[0]
  n_cond = n_prompt + c_sample.shape[0]
  c_all = jnp.concatenate(
      [c_prompt, c_sample, jnp.zeros((MOD_ROWS - n_cond, D_MODEL), F32)], axis=0)
  mod = _adaln(c_all, ada_w, ada_b).reshape(DEPTH, MOD_ROWS, N_MOD, D_MODEL)

  w1 = ffn_w1.astype(BF16)
  w2 = ffn_w2.astype(BF16)
  w_in = mix_w_in.astype(BF16)
  w_out = mix_w_out.astype(BF16)
  wcat = jnp.transpose(sg_ws, (0, 2, 1, 3)).reshape(
      DEPTH, CHUNK, N_HEADS_SG * CHUNK).astype(BF16)
  sbias = jnp.repeat(jnp.transpose(sg_bs, (0, 2, 1)), HEAD_DIM, axis=2)
  sgg = sg_norm_g.reshape(DEPTH, 1, D_SG)
  grp_g = grp_norm_g.reshape(DEPTH, 1, D_MIX)
  fg = final_g.reshape(1, D_MODEL)

  def trunk(x, row0):
    for l in range(DEPTH):
      x = _ffn(x, mod, norm_g, w1, w2, fg, layer=l, which=0, row0=row0, final=False)
      x = _mixer(x, mod, norm_g, w_in, conv_w, sgg, wcat, sbias, grp_g, w_out,
                 layer=l, row0=row0)
      x = _ffn(x, mod, norm_g, w1, w2, fg, layer=l, which=1, row0=row0,
               final=(l == DEPTH - 1))
    return x

  return trunk(x_prompt, 0), trunk(x_sample, n_prompt)
```

```python
import functools

import numpy as np
import jax
import jax.numpy as jnp
from jax import lax
from jax.experimental import pallas as pl
from jax.experimental.pallas import tpu as pltpu

D_MODEL = 1024
DEPTH = 2
HEAD_DIM = 64
N_HEADS_SG = 8
D_CONV = 512
D_SG = 512
D_MIX = D_CONV + D_SG
D_IN = 3 * D_CONV + 2 * D_SG
CHUNK = 128
D_FF = 2816
N_MOD = 9
EPS = 1e-6

BF16 = jnp.bfloat16
F32 = jnp.float32

MXU_COLS_V7X = 256
SUBLANES_F32 = 8
VMEM_LIMIT_BYTES = 56 * 1024 * 1024

TOKEN_TILE = 512
HALO_ROWS = 2 * SUBLANES_F32
FF_COLS = MXU_COLS_V7X
MOD_ROWS = 16
ADA_COLS = 1024
HEADS_PER_GROUP = 4
GROUP_COLS = HEADS_PER_GROUP * HEAD_DIM


def _resident(block_shape, index_map):
  return pl.BlockSpec(block_shape, index_map, pipeline_mode=pl.Buffered(1))


def _mod_norm(x, g, scale, shift):
  ms = jnp.mean(x * x, axis=-1, keepdims=True)
  y = (x * lax.rsqrt(ms + EPS)) * g
  return y * (1.0 + scale) + shift


def _rms(x, g):
  ms = jnp.mean(x * x, axis=-1, keepdims=True)
  return (x * lax.rsqrt(ms + EPS)) * g


def _adaln_kernel(c_ref, w_ref, b_ref, o_ref):
  c = c_ref[...]
  sc = (c * jax.nn.sigmoid(c)).astype(BF16)
  w = w_ref[...].astype(BF16)
  o_ref[...] = jnp.dot(sc, w, preferred_element_type=F32) + b_ref[...]


def _adaln(c_all, ada_w, ada_b):
  n_out = N_MOD * D_MODEL
  return pl.pallas_call(
      _adaln_kernel,
      grid=(DEPTH, n_out // ADA_COLS),
      in_specs=[
          pl.BlockSpec((MOD_ROWS, D_MODEL), lambda l, j: (0, 0)),
          pl.BlockSpec((None, D_MODEL, ADA_COLS), lambda l, j: (l, 0, j)),
          pl.BlockSpec((None, 1, ADA_COLS), lambda l, j: (l, 0, j)),
      ],
      out_specs=pl.BlockSpec((None, MOD_ROWS, ADA_COLS), lambda l, j: (l, 0, j)),
      out_shape=jax.ShapeDtypeStruct((DEPTH, MOD_ROWS, n_out), F32),
      compiler_params=pltpu.CompilerParams(
          dimension_semantics=("arbitrary", "arbitrary"),
          vmem_limit_bytes=VMEM_LIMIT_BYTES),
      name="adaln_mod",
  )(c_all, ada_w, ada_b.reshape(DEPTH, 1, n_out))


def _ffn_kernel(k, final, x_ref, xnext_ref, mod_ref, modnext_ref, ng_ref, w1_ref, w2_ref,
                fg_ref, o_ref, h_scr, a_scr):
  g = ng_ref[k:k + 1, :]

  def first_half(x_r, mod_r):
    x = x_r[...]
    inv = lax.rsqrt(jnp.mean(x * x, axis=-1, keepdims=True) + EPS)
    scale1 = 1.0 + mod_r[3 * k + 1:3 * k + 2, :]
    shift = mod_r[3 * k:3 * k + 1, :]
    for kb in range(D_MODEL // MXU_COLS_V7X):
      cs = slice(kb * MXU_COLS_V7X, (kb + 1) * MXU_COLS_V7X)
      h_scr[:, cs] = (((x_r[:, cs] * inv) * g[:, cs]) * scale1[:, cs]
                      + shift[:, cs]).astype(BF16)
    for c in range(D_FF // FF_COLS):
      lo = c * FF_COLS
      gt = jnp.dot(h_scr[...], w1_ref[:, lo:lo + FF_COLS], preferred_element_type=F32)
      up = jnp.dot(h_scr[...], w1_ref[:, D_FF + lo:D_FF + lo + FF_COLS],
                   preferred_element_type=F32)
      a_scr[:, lo:lo + FF_COLS] = ((gt * jax.nn.sigmoid(gt)) * up).astype(BF16)

  @pl.when(pl.program_id(0) == 0)
  def _():
    first_half(x_ref, mod_ref)

  out = jnp.dot(a_scr[...], w2_ref[...], preferred_element_type=F32)
  y = x_ref[...] + (0.5 * mod_ref[3 * k + 2:3 * k + 3, :]) * out
  if final:
    y = _rms(y, fg_ref[...])
  o_ref[...] = y
  first_half(xnext_ref, modnext_ref)


def _ffn(x, mod, norm_g, w1, w2, final_g, *, layer, which, row0, final):
  bsz, seq, _ = x.shape
  k = 2 * which
  t = TOKEN_TILE
  nt = seq // t
  n = bsz * nt
  nxt = lambda s: jnp.minimum(s + 1, n - 1)
  x2 = x.reshape(bsz * seq, D_MODEL)
  out = pl.pallas_call(
      functools.partial(_ffn_kernel, k, final),
      grid=(n,),
      in_specs=[
          pl.BlockSpec((t, D_MODEL), lambda s: (s, 0)),
          pl.BlockSpec((t, D_MODEL), lambda s: (nxt(s), 0)),
          pl.BlockSpec((None, None, N_MOD, D_MODEL), lambda s: (layer, row0 + s // nt, 0, 0)),
          pl.BlockSpec((None, None, N_MOD, D_MODEL),
                       lambda s: (layer, row0 + nxt(s) // nt, 0, 0)),
          _resident((None, 3, D_MODEL), lambda s: (layer, 0, 0)),
          _resident((None, None, D_MODEL, 2 * D_FF), lambda s: (layer, which, 0, 0)),
          _resident((None, None, D_FF, D_MODEL), lambda s: (layer, which, 0, 0)),
          _resident((1, D_MODEL), lambda s: (0, 0)),
      ],
      out_specs=pl.BlockSpec((t, D_MODEL), lambda s: (s, 0)),
      out_shape=jax.ShapeDtypeStruct((bsz * seq, D_MODEL), F32),
      scratch_shapes=[pltpu.VMEM((t, D_MODEL), BF16), pltpu.VMEM((t, D_FF), BF16)],
      compiler_params=pltpu.CompilerParams(
          dimension_semantics=("arbitrary",),
          vmem_limit_bytes=VMEM_LIMIT_BYTES),
      name=f"ffn_l{layer}_{which}",
  )(x2, x2, mod, mod, norm_g, w1, w2, final_g)
  return out.reshape(bsz, seq, D_MODEL)


def _mixer_kernel(nt, n, x_ref, xp_ref, xn_ref, xres_ref, modp_ref, modb_ref, ng_ref,
                  win_ref, cw_ref, sgg_ref, wcat_ref, sbias_ref, gg_ref, wout_ref, o_ref,
                  h_scr, ya_scr, u_scr, vn_scr, y_scr):
  t = TOKEN_TILE
  s = pl.program_id(0)
  rd = s % 2
  wr = 1 - rd

  @pl.when(s == 0)
  def _():
    for r in (h_scr, ya_scr, u_scr, vn_scr):
      r[...] = jnp.zeros_like(r)

  lane_head = lax.broadcasted_iota(jnp.int32, (CHUNK, GROUP_COLS), 1) // HEAD_DIM
  sv_chunks = []
  for c in range(t // CHUNK):
    groups = []
    for gi in range(N_HEADS_SG // HEADS_PER_GROUP):
      vc = vn_scr[c * CHUNK:(c + 1) * CHUNK, gi * GROUP_COLS:(gi + 1) * GROUP_COLS]
      v_blk = jnp.concatenate(
          [jnp.where(lane_head == j, vc, 0.0).astype(BF16)
           for j in range(HEADS_PER_GROUP)], axis=0)
      w_g = wcat_ref[:, gi * HEADS_PER_GROUP * CHUNK:(gi + 1) * HEADS_PER_GROUP * CHUNK]
      groups.append(jnp.dot(w_g, v_blk, preferred_element_type=F32))
    sv_chunks.append(jnp.concatenate(groups, axis=1) + sbias_ref[...])
  sv = jnp.concatenate(sv_chunks, axis=0)
  y_scr[:, D_CONV:D_MIX] = _rms(u_scr[...] * sv, gg_ref[:, D_CONV:D_MIX]).astype(BF16)
  y_scr[:, 0:D_CONV] = ya_scr[rd]

  sqrt_half = np.float32(np.sqrt(0.5))
  zv = jnp.dot(h_scr[0:t, :], win_ref[:, 3 * D_CONV + D_SG:D_IN],
               preferred_element_type=F32)
  v = 0.5 * zv * (1.0 + lax.erf(zv * sqrt_half))
  mu = jnp.mean(v, axis=-1, keepdims=True)
  dv = v - mu
  vn_scr[...] = ((dv * lax.rsqrt(jnp.mean(dv * dv, axis=-1, keepdims=True) + EPS))
                 * sgg_ref[...])
  zu = jnp.dot(h_scr[0:t, :], win_ref[:, 3 * D_CONV:3 * D_CONV + D_SG],
               preferred_element_type=F32)
  u_scr[...] = 0.5 * zu * (1.0 + lax.erf(zu * sqrt_half))

  ta = jnp.clip(s - 1, 0, n - 1)
  i = ta % nt
  zc = jnp.dot(h_scr[...], win_ref[:, D_CONV:2 * D_CONV], preferred_element_type=F32)
  zx = jnp.dot(h_scr[...], win_ref[:, 2 * D_CONV:3 * D_CONV], preferred_element_type=F32)
  p_ext = zc * zx
  p = p_ext[0:t, :]
  p_prev = jnp.where(i > 0, p_ext[t + SUBLANES_F32 - 1:t + SUBLANES_F32, :], 0.0)
  p_next = jnp.where(i < nt - 1, p_ext[t + SUBLANES_F32:t + SUBLANES_F32 + 1, :], 0.0)
  row = lax.broadcasted_iota(jnp.int32, (t, 1), 0)
  p_m1 = jnp.where(row == 0, p_prev, pltpu.roll(p, 1, axis=0))
  p_p1 = jnp.where(row == t - 1, p_next, pltpu.roll(p, t - 1, axis=0))
  conv = p_m1 * cw_ref[0:1, :] + p * cw_ref[1:2, :] + p_p1 * cw_ref[2:3, :]
  zb = jnp.dot(h_scr[0:t, :], win_ref[:, 0:D_CONV], preferred_element_type=F32)

  out = jnp.dot(y_scr[...], wout_ref[...], preferred_element_type=F32)
  o_ref[...] = xres_ref[...] + modb_ref[5:6, :] * out

  ya_scr[wr] = _rms(zb * conv, gg_ref[:, 0:D_CONV]).astype(BF16)

  g = ng_ref[1:2, :]
  shift = modp_ref[3:4, :]
  scale = modp_ref[4:5, :]
  h_scr[0:t, :] = _mod_norm(x_ref[...], g, scale, shift).astype(BF16)
  x_halo = jnp.concatenate([xp_ref[...], xn_ref[...]], axis=0)
  h_scr[t:t + HALO_ROWS, :] = _mod_norm(x_halo, g, scale, shift).astype(BF16)


def _mixer(x, mod, norm_g, w_in, conv_w, sg_norm_g, wcat, sbias, grp_g, w_out, *,
           layer, row0):
  bsz, seq, _ = x.shape
  t = TOKEN_TILE
  nt = seq // t
  n = bsz * nt
  r = t // SUBLANES_F32
  n_blk = bsz * seq // SUBLANES_F32
  tp = lambda s: jnp.minimum(s, n - 1)
  tb = lambda s: jnp.clip(s - 2, 0, n - 1)
  x2 = x.reshape(bsz * seq, D_MODEL)
  out = pl.pallas_call(
      functools.partial(_mixer_kernel, nt, n),
      grid=(n + 2,),
      in_specs=[
          pl.BlockSpec((t, D_MODEL), lambda s: (tp(s), 0)),
          pl.BlockSpec((SUBLANES_F32, D_MODEL),
                       lambda s: (jnp.maximum(tp(s) * r - 1, 0), 0)),
          pl.BlockSpec((SUBLANES_F32, D_MODEL),
                       lambda s: (jnp.minimum((tp(s) + 1) * r, n_blk - 1), 0)),
          pl.BlockSpec((t, D_MODEL), lambda s: (tb(s), 0)),
          pl.BlockSpec((None, None, N_MOD, D_MODEL),
                       lambda s: (layer, row0 + tp(s) // nt, 0, 0)),
          pl.BlockSpec((None, None, N_MOD, D_MODEL),
                       lambda s: (layer, row0 + tb(s) // nt, 0, 0)),
          _resident((None, 3, D_MODEL), lambda s: (layer, 0, 0)),
          _resident((None, D_MODEL, D_IN), lambda s: (layer, 0, 0)),
          _resident((None, 3, D_CONV), lambda s: (layer, 0, 0)),
          _resident((None, 1, D_SG), lambda s: (layer, 0, 0)),
          _resident((None, CHUNK, N_HEADS_SG * CHUNK), lambda s: (layer, 0, 0)),
          _resident((None, CHUNK, D_SG), lambda s: (layer, 0, 0)),
          _resident((None, 1, D_MIX), lambda s: (layer, 0, 0)),
          _resident((None, D_MIX, D_MODEL), lambda s: (layer, 0, 0)),
      ],
      out_specs=pl.BlockSpec((t, D_MODEL), lambda s: (tb(s), 0)),
      out_shape=jax.ShapeDtypeStruct((bsz * seq, D_MODEL), F32),
      scratch_shapes=[pltpu.VMEM((t + HALO_ROWS, D_MODEL), BF16),
                      pltpu.VMEM((2, t, D_CONV), BF16),
                      pltpu.VMEM((t, D_SG), F32),
                      pltpu.VMEM((t, D_SG), F32),
                      pltpu.VMEM((t, D_MIX), BF16)],
      compiler_params=pltpu.CompilerParams(
          dimension_semantics=("arbitrary",),
          vmem_limit_bytes=VMEM_LIMIT_BYTES),
      name=f"mixer_l{layer}",
  )(x2, x2, x2, x2, mod, mod, norm_g, w_in, conv_w, sg_norm_g, wcat, sbias, grp_g, w_out)
  return out.reshape(bsz, seq, D_MODEL)


def kernel(x_prompt, x_sample, c_prompt, c_sample, ada_w, ada_b, norm_g, ffn_w1, ffn_w2,
           mix_w_in, conv_w, sg_norm_g, sg_ws, sg_bs, grp_norm_g, mix_w_out, final_g):
  n_prompt = c_prompt.shape[0]
  n_cond = n_prompt + c_sample.shape[0]
  c_all = jnp.concatenate(
      [c_prompt, c_sample, jnp.zeros((MOD_ROWS - n_cond, D_MODEL), F32)], axis=0)
  mod = _adaln(c_all, ada_w, ada_b).reshape(DEPTH, MOD_ROWS, N_MOD, D_MODEL)

  w1 = ffn_w1.astype(BF16)
  w2 = ffn_w2.astype(BF16)
  w_in = mix_w_in.astype(BF16)
  w_out = mix_w_out.astype(BF16)
  wcat = jnp.transpose(sg_ws, (0, 2, 1, 3)).reshape(
      DEPTH, CHUNK, N_HEADS_SG * CHUNK).astype(BF16)
  sbias = jnp.repeat(jnp.transpose(sg_bs, (0, 2, 1)), HEAD_DIM, axis=2)
  sgg = sg_norm_g.reshape(DEPTH, 1, D_SG)
  grp_g = grp_norm_g.reshape(DEPTH, 1, D_MIX)
  fg = final_g.reshape(1, D_MODEL)

  def trunk(x, row0):
    for l in range(DEPTH):
      x = _ffn(x, mod, norm_g, w1, w2, fg, layer=l, which=0, row0=row0, final=False)
      x = _mixer(x, mod, norm_g, w_in, conv_w, sgg, wcat, sbias, grp_g, w_out,
                 layer=l, row0=row0)
      x = _ffn(x, mod, norm_g, w1, w2, fg, layer=l, which=1, row0=row0,
               final=(l == DEPTH - 1))
    return x

  return trunk(x_prompt, 0), trunk(x_sample, n_prompt)
```

```python
import functools

import numpy as np
import jax
import jax.numpy as jnp
from jax import lax
from jax.experimental import pallas as pl
from jax.experimental.pallas import tpu as pltpu

D_MODEL = 1024
DEPTH = 2
HEAD_DIM = 64
N_HEADS_SG = 8
D_CONV = 512
D_SG = 512
D_MIX = D_CONV + D_SG
D_IN = 3 * D_CONV + 2 * D_SG
CHUNK = 128
D_FF = 2816
N_MOD = 9
EPS = 1e-6

BF16 = jnp.bfloat16
F32 = jnp.float32

MXU_COLS_V7X = 256
SUBLANES_F32 = 8
VMEM_LIMIT_BYTES = 56 * 1024 * 1024

TOKEN_TILE = 1024
HALO_ROWS = 2 * SUBLANES_F32
FF_COLS = MXU_COLS_V7X
MOD_ROWS = 16
ADA_COLS = 1024
HEADS_PER_GROUP = 4
GROUP_COLS = HEADS_PER_GROUP * HEAD_DIM


def _resident(block_shape, index_map):
  return pl.BlockSpec(block_shape, index_map, pipeline_mode=pl.Buffered(1))


def _mod_norm(x, g, scale, shift):
  ms = jnp.mean(x * x, axis=-1, keepdims=True)
  y = (x * lax.rsqrt(ms + EPS)) * g
  return y * (1.0 + scale) + shift


def _rms(x, g):
  ms = jnp.mean(x * x, axis=-1, keepdims=True)
  return (x * lax.rsqrt(ms + EPS)) * g


def _adaln_kernel(c_ref, w_ref, b_ref, o_ref):
  c = c_ref[...]
  sc = (c * jax.nn.sigmoid(c)).astype(BF16)
  w = w_ref[...].astype(BF16)
  o_ref[...] = jnp.dot(sc, w, preferred_element_type=F32) + b_ref[...]


def _adaln(c_all, ada_w, ada_b):
  n_out = N_MOD * D_MODEL
  return pl.pallas_call(
      _adaln_kernel,
      grid=(DEPTH, n_out // ADA_COLS),
      in_specs=[
          pl.BlockSpec((MOD_ROWS, D_MODEL), lambda l, j: (0, 0)),
          pl.BlockSpec((None, D_MODEL, ADA_COLS), lambda l, j: (l, 0, j)),
          pl.BlockSpec((None, 1, ADA_COLS), lambda l, j: (l, 0, j)),
      ],
      out_specs=pl.BlockSpec((None, MOD_ROWS, ADA_COLS), lambda l, j: (l, 0, j)),
      out_shape=jax.ShapeDtypeStruct((DEPTH, MOD_ROWS, n_out), F32),
      compiler_params=pltpu.CompilerParams(
          dimension_semantics=("arbitrary", "arbitrary"),
          vmem_limit_bytes=VMEM_LIMIT_BYTES),
      name="adaln_mod",
  )(c_all, ada_w, ada_b.reshape(DEPTH, 1, n_out))


def _ffn_kernel(k, final, x_ref, xnext_ref, mod_ref, modnext_ref, ng_ref, w1_ref, w2_ref,
                fg_ref, o_ref, h_scr, a_scr):
  g = ng_ref[k:k + 1, :]

  def first_half(x_r, mod_r):
    x = x_r[...]
    inv = lax.rsqrt(jnp.mean(x * x, axis=-1, keepdims=True) + EPS)
    scale1 = 1.0 + mod_r[3 * k + 1:3 * k + 2, :]
    shift = mod_r[3 * k:3 * k + 1, :]
    for kb in range(D_MODEL // MXU_COLS_V7X):
      cs = slice(kb * MXU_COLS_V7X, (kb + 1) * MXU_COLS_V7X)
      h_scr[:, cs] = (((x_r[:, cs] * inv) * g[:, cs]) * scale1[:, cs]
                      + shift[:, cs]).astype(BF16)
    for c in range(D_FF // FF_COLS):
      lo = c * FF_COLS
      gt = jnp.dot(h_scr[...], w1_ref[:, lo:lo + FF_COLS], preferred_element_type=F32)
      up = jnp.dot(h_scr[...], w1_ref[:, D_FF + lo:D_FF + lo + FF_COLS],
                   preferred_element_type=F32)
      a_scr[:, lo:lo + FF_COLS] = ((gt * jax.nn.sigmoid(gt)) * up).astype(BF16)

  @pl.when(pl.program_id(0) == 0)
  def _():
    first_half(x_ref, mod_ref)

  out = jnp.dot(a_scr[...], w2_ref[...], preferred_element_type=F32)
  y = x_ref[...] + (0.5 * mod_ref[3 * k + 2:3 * k + 3, :]) * out
  if final:
    y = _rms(y, fg_ref[...])
  o_ref[...] = y
  first_half(xnext_ref, modnext_ref)


def _ffn(x, mod, norm_g, w1, w2, final_g, *, layer, which, row0, final):
  bsz, seq, _ = x.shape
  k = 2 * which
  t = TOKEN_TILE
  nt = seq // t
  n = bsz * nt
  nxt = lambda s: jnp.minimum(s + 1, n - 1)
  x2 = x.reshape(bsz * seq, D_MODEL)
  out = pl.pallas_call(
      functools.partial(_ffn_kernel, k, final),
      grid=(n,),
      in_specs=[
          pl.BlockSpec((t, D_MODEL), lambda s: (s, 0)),
          pl.BlockSpec((t, D_MODEL), lambda s: (nxt(s), 0)),
          pl.BlockSpec((None, None, N_MOD, D_MODEL), lambda s: (layer, row0 + s // nt, 0, 0)),
          pl.BlockSpec((None, None, N_MOD, D_MODEL),
                       lambda s: (layer, row0 + nxt(s) // nt, 0, 0)),
          _resident((None, 3, D_MODEL), lambda s: (layer, 0, 0)),
          _resident((None, None, D_MODEL, 2 * D_FF), lambda s: (layer, which, 0, 0)),
          _resident((None, None, D_FF, D_MODEL), lambda s: (layer, which, 0, 0)),
          _resident((1, D_MODEL), lambda s: (0, 0)),
      ],
      out_specs=pl.BlockSpec((t, D_MODEL), lambda s: (s, 0)),
      out_shape=jax.ShapeDtypeStruct((bsz * seq, D_MODEL), F32),
      scratch_shapes=[pltpu.VMEM((t, D_MODEL), BF16), pltpu.VMEM((t, D_FF), BF16)],
      compiler_params=pltpu.CompilerParams(
          dimension_semantics=("arbitrary",),
          vmem_limit_bytes=VMEM_LIMIT_BYTES),
      name=f"ffn_l{layer}_{which}",
  )(x2, x2, mod, mod, norm_g, w1, w2, final_g)
  return out.reshape(bsz, seq, D_MODEL)


def _mixer_kernel(x_ref, xp_ref, xn_ref, mod_ref, ng_ref, win_ref, cw_ref, sgg_ref,
                  wcat_ref, sbias_ref, gg_ref, wout_ref, o_ref, h_scr, y_scr):
  t = TOKEN_TILE
  i = pl.program_id(1)
  n_i = pl.num_programs(1)
  shift = mod_ref[3:4, :]
  scale = mod_ref[4:5, :]
  gate = mod_ref[5:6, :]
  g = ng_ref[1:2, :]

  h_scr[0:t, :] = _mod_norm(x_ref[...], g, scale, shift).astype(BF16)
  x_halo = jnp.concatenate([xp_ref[...], xn_ref[...]], axis=0)
  h_scr[t:t + HALO_ROWS, :] = _mod_norm(x_halo, g, scale, shift).astype(BF16)

  zc = jnp.dot(h_scr[...], win_ref[:, D_CONV:2 * D_CONV], preferred_element_type=F32)
  zx = jnp.dot(h_scr[...], win_ref[:, 2 * D_CONV:3 * D_CONV], preferred_element_type=F32)
  p_ext = zc * zx
  p = p_ext[0:t, :]
  p_prev = jnp.where(i > 0, p_ext[t + SUBLANES_F32 - 1:t + SUBLANES_F32, :], 0.0)
  p_next = jnp.where(i < n_i - 1, p_ext[t + SUBLANES_F32:t + SUBLANES_F32 + 1, :], 0.0)
  row = lax.broadcasted_iota(jnp.int32, (t, 1), 0)
  p_m1 = jnp.where(row == 0, p_prev, pltpu.roll(p, 1, axis=0))
  p_p1 = jnp.where(row == t - 1, p_next, pltpu.roll(p, t - 1, axis=0))
  conv = p_m1 * cw_ref[0:1, :] + p * cw_ref[1:2, :] + p_p1 * cw_ref[2:3, :]
  b_gate = jnp.dot(h_scr[0:t, :], win_ref[:, 0:D_CONV], preferred_element_type=F32)
  y_a = b_gate * conv
  y_scr[:, 0:D_CONV] = _rms(y_a, gg_ref[:, 0:D_CONV]).astype(BF16)

  sqrt_half = np.float32(np.sqrt(0.5))
  zu = jnp.dot(h_scr[0:t, :], win_ref[:, 3 * D_CONV:3 * D_CONV + D_SG],
               preferred_element_type=F32)
  zv = jnp.dot(h_scr[0:t, :], win_ref[:, 3 * D_CONV + D_SG:D_IN],
               preferred_element_type=F32)
  u = 0.5 * zu * (1.0 + lax.erf(zu * sqrt_half))
  v = 0.5 * zv * (1.0 + lax.erf(zv * sqrt_half))
  mu = jnp.mean(v, axis=-1, keepdims=True)
  dv = v - mu
  v = ((dv * lax.rsqrt(jnp.mean(dv * dv, axis=-1, keepdims=True) + EPS))
       * sgg_ref[...])
  lane_head = lax.broadcasted_iota(jnp.int32, (CHUNK, GROUP_COLS), 1) // HEAD_DIM
  sv_chunks = []
  for n in range(t // CHUNK):
    groups = []
    for gi in range(N_HEADS_SG // HEADS_PER_GROUP):
      vc = v[n * CHUNK:(n + 1) * CHUNK, gi * GROUP_COLS:(gi + 1) * GROUP_COLS]
      v_blk = jnp.concatenate(
          [jnp.where(lane_head == j, vc, 0.0).astype(BF16)
           for j in range(HEADS_PER_GROUP)], axis=0)
      w_g = wcat_ref[:, gi * HEADS_PER_GROUP * CHUNK:(gi + 1) * HEADS_PER_GROUP * CHUNK]
      groups.append(jnp.dot(w_g, v_blk, preferred_element_type=F32))
    sv_chunks.append(jnp.concatenate(groups, axis=1) + sbias_ref[...])
  sv = jnp.concatenate(sv_chunks, axis=0)
  y_b = u * sv
  y_scr[:, D_CONV:D_MIX] = _rms(y_b, gg_ref[:, D_CONV:D_MIX]).astype(BF16)

  out = (jnp.dot(y_scr[:, 0:D_CONV], wout_ref[0:D_CONV, :], preferred_element_type=F32)
         + jnp.dot(y_scr[:, D_CONV:D_MIX], wout_ref[D_CONV:D_MIX, :],
                   preferred_element_type=F32))
  o_ref[...] = x_ref[...] + gate * out


def _mixer(x, mod, norm_g, w_in, conv_w, sg_norm_g, wcat, sbias, grp_g, w_out, *,
           layer, row0):
  bsz, seq, _ = x.shape
  t = TOKEN_TILE
  r = t // SUBLANES_F32
  n_blk = seq // SUBLANES_F32
  return pl.pallas_call(
      _mixer_kernel,
      grid=(bsz, seq // t),
      in_specs=[
          pl.BlockSpec((None, t, D_MODEL), lambda b, i: (b, i, 0)),
          pl.BlockSpec((None, SUBLANES_F32, D_MODEL),
                       lambda b, i: (b, jnp.maximum(i * r - 1, 0), 0)),
          pl.BlockSpec((None, SUBLANES_F32, D_MODEL),
                       lambda b, i: (b, jnp.minimum((i + 1) * r, n_blk - 1), 0)),
          pl.BlockSpec((None, None, N_MOD, D_MODEL), lambda b, i: (layer, row0 + b, 0, 0)),
          _resident((None, 3, D_MODEL), lambda b, i: (layer, 0, 0)),
          _resident((None, D_MODEL, D_IN), lambda b, i: (layer, 0, 0)),
          _resident((None, 3, D_CONV), lambda b, i: (layer, 0, 0)),
          _resident((None, 1, D_SG), lambda b, i: (layer, 0, 0)),
          _resident((None, CHUNK, N_HEADS_SG * CHUNK), lambda b, i: (layer, 0, 0)),
          _resident((None, CHUNK, D_SG), lambda b, i: (layer, 0, 0)),
          _resident((None, 1, D_MIX), lambda b, i: (layer, 0, 0)),
          _resident((None, D_MIX, D_MODEL), lambda b, i: (layer, 0, 0)),
      ],
      out_specs=pl.BlockSpec((None, t, D_MODEL), lambda b, i: (b, i, 0)),
      out_shape=jax.ShapeDtypeStruct(x.shape, F32),
      scratch_shapes=[pltpu.VMEM((t + HALO_ROWS, D_MODEL), BF16),
                      pltpu.VMEM((t, D_MIX), BF16)],
      compiler_params=pltpu.CompilerParams(
          dimension_semantics=("arbitrary", "arbitrary"),
          vmem_limit_bytes=VMEM_LIMIT_BYTES),
      name=f"mixer_l{layer}",
  )(x, x, x, mod, norm_g, w_in, conv_w, sg_norm_g, wcat, sbias, grp_g, w_out)


def kernel(x_prompt, x_sample, c_prompt, c_sample, ada_w, ada_b, norm_g, ffn_w1, ffn_w2,
           mix_w_in, conv_w, sg_norm_g, sg_ws, sg_bs, grp_norm_g, mix_w_out, final_g):
  n_prompt = c_prompt.shape[0]
  n_cond = n_prompt + c_sample.shape[0]
  c_all = jnp.concatenate(
      [c_prompt, c_sample, jnp.zeros((MOD_ROWS - n_cond, D_MODEL), F32)], axis=0)
  mod = _adaln(c_all, ada_w, ada_b).reshape(DEPTH, MOD_ROWS, N_MOD, D_MODEL)

  w1 = ffn_w1.astype(BF16)
  w2 = ffn_w2.astype(BF16)
  w_in = mix_w_in.astype(BF16)
  w_out = mix_w_out.astype(BF16)
  wcat = jnp.transpose(sg_ws, (0, 2, 1, 3)).reshape(
      DEPTH, CHUNK, N_HEADS_SG * CHUNK).astype(BF16)
  sbias = jnp.repeat(jnp.transpose(sg_bs, (0, 2, 1)), HEAD_DIM, axis=2)
  sgg = sg_norm_g.reshape(DEPTH, 1, D_SG)
  grp_g = grp_norm_g.reshape(DEPTH, 1, D_MIX)
  fg = final_g.reshape(1, D_MODEL)

  def trunk(x, row0):
    for l in range(DEPTH):
      x = _ffn(x, mod, norm_g, w1, w2, fg, layer=l, which=0, row0=row0, final=False)
      x = _mixer(x, mod, norm_g, w_in, conv_w, sgg, wcat, sbias, grp_g, w_out,
                 layer=l, row0=row0)
      x = _ffn(x, mod, norm_g, w1, w2, fg, layer=l, which=1, row0=row0,
               final=(l == DEPTH - 1))
    return x

  return trunk(x_prompt, 0), trunk(x_sample, n_prompt)
```

```python
import functools

import numpy as np
import jax
import jax.numpy as jnp
from jax import lax
from jax.experimental import pallas as pl
from jax.experimental.pallas import tpu as pltpu

D_MODEL = 1024
DEPTH = 2
HEAD_DIM = 64
N_HEADS_SG = 8
D_CONV = 512
D_SG = 512
D_MIX = D_CONV + D_SG
D_IN = 3 * D_CONV + 2 * D_SG
CHUNK = 128
D_FF = 2816
N_MOD = 9
EPS = 1e-6

BF16 = jnp.bfloat16
F32 = jnp.float32

MXU_COLS_V7X = 256
SUBLANES_F32 = 8
VMEM_LIMIT_BYTES = 56 * 1024 * 1024

FFN_TILE = 512
MIX_TILE = 1024
HALO_ROWS = 2 * SUBLANES_F32
FF_COLS = MXU_COLS_V7X
MOD_ROWS = 16
ADA_COLS = 1024
HEADS_PER_GROUP = 4
GROUP_COLS = HEADS_PER_GROUP * HEAD_DIM


def _resident(block_shape, index_map):
  return pl.BlockSpec(block_shape, index_map, pipeline_mode=pl.Buffered(1))


def _mod_norm(x, g, scale, shift):
  ms = jnp.mean(x * x, axis=-1, keepdims=True)
  y = (x * lax.rsqrt(ms + EPS)) * g
  return y * (1.0 + scale) + shift


def _rms(x, g):
  ms = jnp.mean(x * x, axis=-1, keepdims=True)
  return (x * lax.rsqrt(ms + EPS)) * g


def _adaln_kernel(c_ref, w_ref, b_ref, o_ref):
  c = c_ref[...]
  sc = (c * jax.nn.sigmoid(c)).astype(BF16)
  w = w_ref[...].astype(BF16)
  o_ref[...] = jnp.dot(sc, w, preferred_element_type=F32) + b_ref[...]


def _adaln(c_all, ada_w, ada_b):
  n_out = N_MOD * D_MODEL
  return pl.pallas_call(
      _adaln_kernel,
      grid=(DEPTH, n_out // ADA_COLS),
      in_specs=[
          pl.BlockSpec((MOD_ROWS, D_MODEL), lambda l, j: (0, 0)),
          pl.BlockSpec((None, D_MODEL, ADA_COLS), lambda l, j: (l, 0, j)),
          pl.BlockSpec((None, 1, ADA_COLS), lambda l, j: (l, 0, j)),
      ],
      out_specs=pl.BlockSpec((None, MOD_ROWS, ADA_COLS), lambda l, j: (l, 0, j)),
      out_shape=jax.ShapeDtypeStruct((DEPTH, MOD_ROWS, n_out), F32),
      compiler_params=pltpu.CompilerParams(
          dimension_semantics=("arbitrary", "arbitrary"),
          vmem_limit_bytes=VMEM_LIMIT_BYTES),
      name="adaln_mod",
  )(c_all, ada_w, ada_b.reshape(DEPTH, 1, n_out))


def _ffn_kernel(k, final, x_ref, xnext_ref, mod_ref, modnext_ref, ng_ref, w1_ref, w2_ref,
                fg_ref, o_ref, h_scr, a_scr):
  g = ng_ref[k:k + 1, :]

  def first_half(x_r, mod_r):
    x = x_r[...]
    inv = lax.rsqrt(jnp.mean(x * x, axis=-1, keepdims=True) + EPS)
    scale1 = 1.0 + mod_r[3 * k + 1:3 * k + 2, :]
    shift = mod_r[3 * k:3 * k + 1, :]
    for kb in range(D_MODEL // MXU_COLS_V7X):
      cs = slice(kb * MXU_COLS_V7X, (kb + 1) * MXU_COLS_V7X)
      h_scr[:, cs] = (((x_r[:, cs] * inv) * g[:, cs]) * scale1[:, cs]
                      + shift[:, cs]).astype(BF16)
    for c in range(D_FF // FF_COLS):
      lo = c * FF_COLS
      gt = jnp.dot(h_scr[...], w1_ref[:, lo:lo + FF_COLS], preferred_element_type=F32)
      up = jnp.dot(h_scr[...], w1_ref[:, D_FF + lo:D_FF + lo + FF_COLS],
                   preferred_element_type=F32)
      a_scr[:, lo:lo + FF_COLS] = ((gt * jax.nn.sigmoid(gt)) * up).astype(BF16)

  @pl.when(pl.program_id(0) == 0)
  def _():
    first_half(x_ref, mod_ref)

  out = jnp.dot(a_scr[...], w2_ref[...], preferred_element_type=F32)
  y = x_ref[...] + (0.5 * mod_ref[3 * k + 2:3 * k + 3, :]) * out
  if final:
    y = _rms(y, fg_ref[...])
  o_ref[...] = y
  first_half(xnext_ref, modnext_ref)


def _ffn(x, mod, norm_g, w1, w2, final_g, *, layer, which, row0, final):
  bsz, seq, _ = x.shape
  k = 2 * which
  t = FFN_TILE
  nt = seq // t
  n = bsz * nt
  nxt = lambda s: jnp.minimum(s + 1, n - 1)
  x2 = x.reshape(bsz * seq, D_MODEL)
  out = pl.pallas_call(
      functools.partial(_ffn_kernel, k, final),
      grid=(n,),
      in_specs=[
          pl.BlockSpec((t, D_MODEL), lambda s: (s, 0)),
          pl.BlockSpec((t, D_MODEL), lambda s: (nxt(s), 0)),
          pl.BlockSpec((None, None, N_MOD, D_MODEL), lambda s: (layer, row0 + s // nt, 0, 0)),
          pl.BlockSpec((None, None, N_MOD, D_MODEL),
                       lambda s: (layer, row0 + nxt(s) // nt, 0, 0)),
          _resident((None, 3, D_MODEL), lambda s: (layer, 0, 0)),
          _resident((None, None, D_MODEL, 2 * D_FF), lambda s: (layer, which, 0, 0)),
          _resident((None, None, D_FF, D_MODEL), lambda s: (layer, which, 0, 0)),
          _resident((1, D_MODEL), lambda s: (0, 0)),
      ],
      out_specs=pl.BlockSpec((t, D_MODEL), lambda s: (s, 0)),
      out_shape=jax.ShapeDtypeStruct((bsz * seq, D_MODEL), F32),
      scratch_shapes=[pltpu.VMEM((t, D_MODEL), BF16), pltpu.VMEM((t, D_FF), BF16)],
      compiler_params=pltpu.CompilerParams(
          dimension_semantics=("arbitrary",),
          vmem_limit_bytes=VMEM_LIMIT_BYTES),
      name=f"ffn_l{layer}_{which}",
  )(x2, x2, mod, mod, norm_g, w1, w2, final_g)
  return out.reshape(bsz, seq, D_MODEL)


def _mixer_kernel(x_ref, xp_ref, xn_ref, mod_ref, ng_ref, win_ref, cw_ref, sgg_ref,
                  wcat_ref, sbias_ref, gg_ref, wout_ref, o_ref, h_scr, y_scr):
  t = MIX_TILE
  i = pl.program_id(1)
  n_i = pl.num_programs(1)
  shift = mod_ref[3:4, :]
  scale = mod_ref[4:5, :]
  gate = mod_ref[5:6, :]
  g = ng_ref[1:2, :]

  h_scr[0:t, :] = _mod_norm(x_ref[...], g, scale, shift).astype(BF16)
  x_halo = jnp.concatenate([xp_ref[...], xn_ref[...]], axis=0)
  h_scr[t:t + HALO_ROWS, :] = _mod_norm(x_halo, g, scale, shift).astype(BF16)

  zc = jnp.dot(h_scr[...], win_ref[:, D_CONV:2 * D_CONV], preferred_element_type=F32)
  zx = jnp.dot(h_scr[...], win_ref[:, 2 * D_CONV:3 * D_CONV], preferred_element_type=F32)
  p_ext = zc * zx
  p = p_ext[0:t, :]
  p_prev = jnp.where(i > 0, p_ext[t + SUBLANES_F32 - 1:t + SUBLANES_F32, :], 0.0)
  p_next = jnp.where(i < n_i - 1, p_ext[t + SUBLANES_F32:t + SUBLANES_F32 + 1, :], 0.0)
  row = lax.broadcasted_iota(jnp.int32, (t, 1), 0)
  p_m1 = jnp.where(row == 0, p_prev, pltpu.roll(p, 1, axis=0))
  p_p1 = jnp.where(row == t - 1, p_next, pltpu.roll(p, t - 1, axis=0))
  conv = p_m1 * cw_ref[0:1, :] + p * cw_ref[1:2, :] + p_p1 * cw_ref[2:3, :]
  b_gate = jnp.dot(h_scr[0:t, :], win_ref[:, 0:D_CONV], preferred_element_type=F32)
  y_a = b_gate * conv
  y_scr[:, 0:D_CONV] = _rms(y_a, gg_ref[:, 0:D_CONV]).astype(BF16)

  sqrt_half = np.float32(np.sqrt(0.5))
  zu = jnp.dot(h_scr[0:t, :], win_ref[:, 3 * D_CONV:3 * D_CONV + D_SG],
               preferred_element_type=F32)
  zv = jnp.dot(h_scr[0:t, :], win_ref[:, 3 * D_CONV + D_SG:D_IN],
               preferred_element_type=F32)
  u = 0.5 * zu * (1.0 + lax.erf(zu * sqrt_half))
  v = 0.5 * zv * (1.0 + lax.erf(zv * sqrt_half))
  mu = jnp.mean(v, axis=-1, keepdims=True)
  dv = v - mu
  v = ((dv * lax.rsqrt(jnp.mean(dv * dv, axis=-1, keepdims=True) + EPS))
       * sgg_ref[...])
  lane_head = lax.broadcasted_iota(jnp.int32, (CHUNK, GROUP_COLS), 1) // HEAD_DIM
  sv_chunks = []
  for n in range(t // CHUNK):
    groups = []
    for gi in range(N_HEADS_SG // HEADS_PER_GROUP):
      vc = v[n * CHUNK:(n + 1) * CHUNK, gi * GROUP_COLS:(gi + 1) * GROUP_COLS]
      v_blk = jnp.concatenate(
          [jnp.where(lane_head == j, vc, 0.0).astype(BF16)
           for j in range(HEADS_PER_GROUP)], axis=0)
      w_g = wcat_ref[:, gi * HEADS_PER_GROUP * CHUNK:(gi + 1) * HEADS_PER_GROUP * CHUNK]
      groups.append(jnp.dot(w_g, v_blk, preferred_element_type=F32))
    sv_chunks.append(jnp.concatenate(groups, axis=1) + sbias_ref[...])
  sv = jnp.concatenate(sv_chunks, axis=0)
  y_b = u * sv
  y_scr[:, D_CONV:D_MIX] = _rms(y_b, gg_ref[:, D_CONV:D_MIX]).astype(BF16)

  out = (jnp.dot(y_scr[:, 0:D_CONV], wout_ref[0:D_CONV, :], preferred_element_type=F32)
         + jnp.dot(y_scr[:, D_CONV:D_MIX], wout_ref[D_CONV:D_MIX, :],
                   preferred_element_type=F32))
  o_ref[...] = x_ref[...] + gate * out


def _mixer(x, mod, norm_g, w_in, conv_w, sg_norm_g, wcat, sbias, grp_g, w_out, *,
           layer, row0):
  bsz, seq, _ = x.shape
  t = MIX_TILE
  r = t // SUBLANES_F32
  n_blk = seq // SUBLANES_F32
  return pl.pallas_call(
      _mixer_kernel,
      grid=(bsz, seq // t),
      in_specs=[
          pl.BlockSpec((None, t, D_MODEL), lambda b, i: (b, i, 0)),
          pl.BlockSpec((None, SUBLANES_F32, D_MODEL),
                       lambda b, i: (b, jnp.maximum(i * r - 1, 0), 0)),
          pl.BlockSpec((None, SUBLANES_F32, D_MODEL),
                       lambda b, i: (b, jnp.minimum((i + 1) * r, n_blk - 1), 0)),
          pl.BlockSpec((None, None, N_MOD, D_MODEL), lambda b, i: (layer, row0 + b, 0, 0)),
          _resident((None, 3, D_MODEL), lambda b, i: (layer, 0, 0)),
          _resident((None, D_MODEL, D_IN), lambda b, i: (layer, 0, 0)),
          _resident((None, 3, D_CONV), lambda b, i: (layer, 0, 0)),
          _resident((None, 1, D_SG), lambda b, i: (layer, 0, 0)),
          _resident((None, CHUNK, N_HEADS_SG * CHUNK), lambda b, i: (layer, 0, 0)),
          _resident((None, CHUNK, D_SG), lambda b, i: (layer, 0, 0)),
          _resident((None, 1, D_MIX), lambda b, i: (layer, 0, 0)),
          _resident((None, D_MIX, D_MODEL), lambda b, i: (layer, 0, 0)),
      ],
      out_specs=pl.BlockSpec((None, t, D_MODEL), lambda b, i: (b, i, 0)),
      out_shape=jax.ShapeDtypeStruct(x.shape, F32),
      scratch_shapes=[pltpu.VMEM((t + HALO_ROWS, D_MODEL), BF16),
                      pltpu.VMEM((t, D_MIX), BF16)],
      compiler_params=pltpu.CompilerParams(
          dimension_semantics=("arbitrary", "arbitrary"),
          vmem_limit_bytes=VMEM_LIMIT_BYTES),
      name=f"mixer_l{layer}",
  )(x, x, x, mod, norm_g, w_in, conv_w, sg_norm_g, wcat, sbias, grp_g, w_out)


def kernel(x_prompt, x_sample, c_prompt, c_sample, ada_w, ada_b, norm_g, ffn_w1, ffn_w2,
           mix_w_in, conv_w, sg_norm_g, sg_ws, sg_bs, grp_norm_g, mix_w_out, final_g):
  n_prompt = c_prompt.shape[0]
  n_cond = n_prompt + c_sample.shape[0]
  c_all = jnp.concatenate(
      [c_prompt, c_sample, jnp.zeros((MOD_ROWS - n_cond, D_MODEL), F32)], axis=0)
  mod = _adaln(c_all, ada_w, ada_b).reshape(DEPTH, MOD_ROWS, N_MOD, D_MODEL)

  w1 = ffn_w1.astype(BF16)
  w2 = ffn_w2.astype(BF16)
  w_in = mix_w_in.astype(BF16)
  w_out = mix_w_out.astype(BF16)
  wcat = jnp.transpose(sg_ws, (0, 2, 1, 3)).reshape(
      DEPTH, CHUNK, N_HEADS_SG * CHUNK).astype(BF16)
  sbias = jnp.repeat(jnp.transpose(sg_bs, (0, 2, 1)), HEAD_DIM, axis=2)
  sgg = sg_norm_g.reshape(DEPTH, 1, D_SG)
  grp_g = grp_norm_g.reshape(DEPTH, 1, D_MIX)
  fg = final_g.reshape(1, D_MODEL)

  def trunk(x, row0):
    for l in range(DEPTH):
      x = _ffn(x, mod, norm_g, w1, w2, fg, layer=l, which=0, row0=row0, final=False)
      x = _mixer(x, mod, norm_g, w_in, conv_w, sgg, wcat, sbias, grp_g, w_out,
                 layer=l, row0=row0)
      x = _ffn(x, mod, norm_g, w1, w2, fg, layer=l, which=1, row0=row0,
               final=(l == DEPTH - 1))
    return x

  return trunk(x_prompt, 0), trunk(x_sample, n_prompt)
```

```python
import functools

import numpy as np
import jax
import jax.numpy as jnp
from jax import lax
from jax.experimental import pallas as pl
from jax.experimental.pallas import tpu as pltpu

D_MODEL = 1024
DEPTH = 2
HEAD_DIM = 64
N_HEADS_SG = 8
D_CONV = 512
D_SG = 512
D_MIX = D_CONV + D_SG
D_IN = 3 * D_CONV + 2 * D_SG
CHUNK = 128
D_FF = 2816
N_MOD = 9
EPS = 1e-6

BF16 = jnp.bfloat16
F32 = jnp.float32

MXU_COLS_V7X = 256
SUBLANES_F32 = 8
VMEM_LIMIT_BYTES = 56 * 1024 * 1024

FFN_TILE = 512
MIX_TILE = 1024
HALO_ROWS = 2 * SUBLANES_F32
FF_COLS = MXU_COLS_V7X
MOD_ROWS = 16
ADA_COLS = 1024
HEADS_PER_GROUP = 4
GROUP_COLS = HEADS_PER_GROUP * HEAD_DIM


def _resident(block_shape, index_map):
  return pl.BlockSpec(block_shape, index_map, pipeline_mode=pl.Buffered(1))


def _mod_norm(x, g, scale, shift):
  ms = jnp.mean(x * x, axis=-1, keepdims=True)
  y = (x * lax.rsqrt(ms + EPS)) * g
  return y * (1.0 + scale) + shift


def _rms(x, g):
  ms = jnp.mean(x * x, axis=-1, keepdims=True)
  return (x * lax.rsqrt(ms + EPS)) * g


class _Layout:
  def __init__(self, groups, tile, first_row=0):
    self.tiles_per_seq = [seq // tile for _, seq in groups]
    self.tile0, self.row0 = [], []
    tile0, row0 = 0, first_row
    for (bsz, _), tps in zip(groups, self.tiles_per_seq):
      self.tile0.append(tile0)
      self.row0.append(row0)
      tile0 += bsz * tps
      row0 += bsz
    self.n_tiles = tile0

  def _pick(self, tau, per_group):
    out = per_group(0)
    for g in range(1, len(self.tile0)):
      out = jnp.where(tau >= self.tile0[g], per_group(g), out)
    return out

  def row(self, tau):
    return self._pick(
        tau, lambda g: self.row0[g] + (tau - self.tile0[g]) // self.tiles_per_seq[g])

  def is_first(self, tau):
    return self._pick(tau, lambda g: (tau - self.tile0[g]) % self.tiles_per_seq[g] == 0)

  def is_last(self, tau):
    return self._pick(
        tau,
        lambda g: (tau - self.tile0[g]) % self.tiles_per_seq[g] == self.tiles_per_seq[g] - 1)


def _adaln_kernel(c_ref, w_ref, b_ref, o_ref):
  c = c_ref[...]
  sc = (c * jax.nn.sigmoid(c)).astype(BF16)
  w = w_ref[...].astype(BF16)
  o_ref[...] = jnp.dot(sc, w, preferred_element_type=F32) + b_ref[...]


def _adaln(c_all, ada_w, ada_b):
  n_out = N_MOD * D_MODEL
  return pl.pallas_call(
      _adaln_kernel,
      grid=(DEPTH, n_out // ADA_COLS),
      in_specs=[
          pl.BlockSpec((MOD_ROWS, D_MODEL), lambda l, j: (0, 0)),
          pl.BlockSpec((None, D_MODEL, ADA_COLS), lambda l, j: (l, 0, j)),
          pl.BlockSpec((None, 1, ADA_COLS), lambda l, j: (l, 0, j)),
      ],
      out_specs=pl.BlockSpec((None, MOD_ROWS, ADA_COLS), lambda l, j: (l, 0, j)),
      out_shape=jax.ShapeDtypeStruct((DEPTH, MOD_ROWS, n_out), F32),
      compiler_params=pltpu.CompilerParams(
          dimension_semantics=("arbitrary", "arbitrary"),
          vmem_limit_bytes=VMEM_LIMIT_BYTES),
      name="adaln_mod",
  )(c_all, ada_w, ada_b.reshape(DEPTH, 1, n_out))


def _ffn_kernel(k, final, x_ref, xnext_ref, mod_ref, modnext_ref, ng_ref, w1_ref, w2_ref,
                fg_ref, *rest):
  o_ref, h_scr, a_scr = rest[-3:]
  g = ng_ref[k:k + 1, :]

  def first_half(x_r, mod_r):
    x = x_r[...]
    inv = lax.rsqrt(jnp.mean(x * x, axis=-1, keepdims=True) + EPS)
    scale1 = 1.0 + mod_r[3 * k + 1:3 * k + 2, :]
    shift = mod_r[3 * k:3 * k + 1, :]
    for kb in range(D_MODEL // MXU_COLS_V7X):
      cs = slice(kb * MXU_COLS_V7X, (kb + 1) * MXU_COLS_V7X)
      h_scr[:, cs] = (((x_r[:, cs] * inv) * g[:, cs]) * scale1[:, cs]
                      + shift[:, cs]).astype(BF16)
    for c in range(D_FF // FF_COLS):
      lo = c * FF_COLS
      gt = jnp.dot(h_scr[...], w1_ref[:, lo:lo + FF_COLS], preferred_element_type=F32)
      up = jnp.dot(h_scr[...], w1_ref[:, D_FF + lo:D_FF + lo + FF_COLS],
                   preferred_element_type=F32)
      a_scr[:, lo:lo + FF_COLS] = ((gt * jax.nn.sigmoid(gt)) * up).astype(BF16)

  @pl.when(pl.program_id(0) == 0)
  def _():
    first_half(x_ref, mod_ref)

  out = jnp.dot(a_scr[...], w2_ref[...], preferred_element_type=F32)
  y = x_ref[...] + (0.5 * mod_ref[3 * k + 2:3 * k + 3, :]) * out
  if final:
    y = _rms(y, fg_ref[...])
  o_ref[...] = y
  first_half(xnext_ref, modnext_ref)


def _ffn(x2, mod, norm_g, w1, w2, final_g, *, layer, which, final, lay, in_tile0=0,
         out_rows, out_tile0=0, dst=None):
  k = 2 * which
  t = FFN_TILE
  n = lay.n_tiles
  nxt = lambda s: jnp.minimum(s + 1, n - 1)
  in_specs = [
      pl.BlockSpec((t, D_MODEL), lambda s: (in_tile0 + s, 0)),
      pl.BlockSpec((t, D_MODEL), lambda s: (in_tile0 + nxt(s), 0)),
      pl.BlockSpec((None, None, N_MOD, D_MODEL), lambda s: (layer, lay.row(s), 0, 0)),
      pl.BlockSpec((None, None, N_MOD, D_MODEL), lambda s: (layer, lay.row(nxt(s)), 0, 0)),
      _resident((None, 3, D_MODEL), lambda s: (layer, 0, 0)),
      _resident((None, None, D_MODEL, 2 * D_FF), lambda s: (layer, which, 0, 0)),
      _resident((None, None, D_FF, D_MODEL), lambda s: (layer, which, 0, 0)),
      _resident((1, D_MODEL), lambda s: (0, 0)),
  ]
  operands = [x2, x2, mod, mod, norm_g, w1, w2, final_g]
  aliases = {}
  if dst is not None:
    in_specs.append(pl.BlockSpec(memory_space=pl.ANY))
    operands.append(dst)
    aliases = {len(operands) - 1: 0}
  return pl.pallas_call(
      functools.partial(_ffn_kernel, k, final),
      grid=(n,),
      in_specs=in_specs,
      out_specs=pl.BlockSpec((t, D_MODEL), lambda s: (out_tile0 + s, 0)),
      out_shape=jax.ShapeDtypeStruct((out_rows, D_MODEL), F32),
      scratch_shapes=[pltpu.VMEM((t, D_MODEL), BF16), pltpu.VMEM((t, D_FF), BF16)],
      input_output_aliases=aliases,
      compiler_params=pltpu.CompilerParams(
          dimension_semantics=("arbitrary",),
          vmem_limit_bytes=VMEM_LIMIT_BYTES),
      name=f"ffn_l{layer}_{which}",
  )(*operands)


def _mixer_kernel(lay, x_ref, xp_ref, xn_ref, mod_ref, ng_ref, win_ref, cw_ref, sgg_ref,
                  wcat_ref, sbias_ref, gg_ref, wout_ref, o_ref, h_scr, y_scr):
  t = MIX_TILE
  tau = pl.program_id(0)
  shift = mod_ref[3:4, :]
  scale = mod_ref[4:5, :]
  gate = mod_ref[5:6, :]
  g = ng_ref[1:2, :]

  h_scr[0:t, :] = _mod_norm(x_ref[...], g, scale, shift).astype(BF16)
  x_halo = jnp.concatenate([xp_ref[...], xn_ref[...]], axis=0)
  h_scr[t:t + HALO_ROWS, :] = _mod_norm(x_halo, g, scale, shift).astype(BF16)

  zc = jnp.dot(h_scr[...], win_ref[:, D_CONV:2 * D_CONV], preferred_element_type=F32)
  zx = jnp.dot(h_scr[...], win_ref[:, 2 * D_CONV:3 * D_CONV], preferred_element_type=F32)
  p_ext = zc * zx
  p = p_ext[0:t, :]
  p_prev = jnp.where(lay.is_first(tau), 0.0,
                     p_ext[t + SUBLANES_F32 - 1:t + SUBLANES_F32, :])
  p_next = jnp.where(lay.is_last(tau), 0.0,
                     p_ext[t + SUBLANES_F32:t + SUBLANES_F32 + 1, :])
  row = lax.broadcasted_iota(jnp.int32, (t, 1), 0)
  p_m1 = jnp.where(row == 0, p_prev, pltpu.roll(p, 1, axis=0))
  p_p1 = jnp.where(row == t - 1, p_next, pltpu.roll(p, t - 1, axis=0))
  conv = p_m1 * cw_ref[0:1, :] + p * cw_ref[1:2, :] + p_p1 * cw_ref[2:3, :]
  b_gate = jnp.dot(h_scr[0:t, :], win_ref[:, 0:D_CONV], preferred_element_type=F32)
  y_a = b_gate * conv
  y_scr[:, 0:D_CONV] = _rms(y_a, gg_ref[:, 0:D_CONV]).astype(BF16)

  sqrt_half = np.float32(np.sqrt(0.5))
  zu = jnp.dot(h_scr[0:t, :], win_ref[:, 3 * D_CONV:3 * D_CONV + D_SG],
               preferred_element_type=F32)
  zv = jnp.dot(h_scr[0:t, :], win_ref[:, 3 * D_CONV + D_SG:D_IN],
               preferred_element_type=F32)
  u = 0.5 * zu * (1.0 + lax.erf(zu * sqrt_half))
  v = 0.5 * zv * (1.0 + lax.erf(zv * sqrt_half))
  mu = jnp.mean(v, axis=-1, keepdims=True)
  dv = v - mu
  v = ((dv * lax.rsqrt(jnp.mean(dv * dv, axis=-1, keepdims=True) + EPS))
       * sgg_ref[...])
  lane_head = lax.broadcasted_iota(jnp.int32, (CHUNK, GROUP_COLS), 1) // HEAD_DIM
  sv_chunks = []
  for n in range(t // CHUNK):
    groups = []
    for gi in range(N_HEADS_SG // HEADS_PER_GROUP):
      vc = v[n * CHUNK:(n + 1) * CHUNK, gi * GROUP_COLS:(gi + 1) * GROUP_COLS]
      v_blk = jnp.concatenate(
          [jnp.where(lane_head == j, vc, 0.0).astype(BF16)
           for j in range(HEADS_PER_GROUP)], axis=0)
      w_g = wcat_ref[:, gi * HEADS_PER_GROUP * CHUNK:(gi + 1) * HEADS_PER_GROUP * CHUNK]
      groups.append(jnp.dot(w_g, v_blk, preferred_element_type=F32))
    sv_chunks.append(jnp.concatenate(groups, axis=1) + sbias_ref[...])
  sv = jnp.concatenate(sv_chunks, axis=0)
  y_b = u * sv
  y_scr[:, D_CONV:D_MIX] = _rms(y_b, gg_ref[:, D_CONV:D_MIX]).astype(BF16)

  out = (jnp.dot(y_scr[:, 0:D_CONV], wout_ref[0:D_CONV, :], preferred_element_type=F32)
         + jnp.dot(y_scr[:, D_CONV:D_MIX], wout_ref[D_CONV:D_MIX, :],
                   preferred_element_type=F32))
  o_ref[...] = x_ref[...] + gate * out


def _mixer(x2, mod, norm_g, w_in, conv_w, sg_norm_g, wcat, sbias, grp_g, w_out, *,
           layer, lay):
  t = MIX_TILE
  r = t // SUBLANES_F32
  n_blk = x2.shape[0] // SUBLANES_F32
  return pl.pallas_call(
      functools.partial(_mixer_kernel, lay),
      grid=(lay.n_tiles,),
      in_specs=[
          pl.BlockSpec((t, D_MODEL), lambda s: (s, 0)),
          pl.BlockSpec((SUBLANES_F32, D_MODEL), lambda s: (jnp.maximum(s * r - 1, 0), 0)),
          pl.BlockSpec((SUBLANES_F32, D_MODEL),
                       lambda s: (jnp.minimum((s + 1) * r, n_blk - 1), 0)),
          pl.BlockSpec((None, None, N_MOD, D_MODEL), lambda s: (layer, lay.row(s), 0, 0)),
          _resident((None, 3, D_MODEL), lambda s: (layer, 0, 0)),
          _resident((None, D_MODEL, D_IN), lambda s: (layer, 0, 0)),
          _resident((None, 3, D_CONV), lambda s: (layer, 0, 0)),
          _resident((None, 1, D_SG), lambda s: (layer, 0, 0)),
          _resident((None, CHUNK, N_HEADS_SG * CHUNK), lambda s: (layer, 0, 0)),
          _resident((None, CHUNK, D_SG), lambda s: (layer, 0, 0)),
          _resident((None, 1, D_MIX), lambda s: (layer, 0, 0)),
          _resident((None, D_MIX, D_MODEL), lambda s: (layer, 0, 0)),
      ],
      out_specs=pl.BlockSpec((t, D_MODEL), lambda s: (s, 0)),
      out_shape=jax.ShapeDtypeStruct(x2.shape, F32),
      scratch_shapes=[pltpu.VMEM((t + HALO_ROWS, D_MODEL), BF16),
                      pltpu.VMEM((t, D_MIX), BF16)],
      compiler_params=pltpu.CompilerParams(
          dimension_semantics=("arbitrary",),
          vmem_limit_bytes=VMEM_LIMIT_BYTES),
      name=f"mixer_l{layer}",
  )(x2, x2, x2, mod, norm_g, w_in, conv_w, sg_norm_g, wcat, sbias, grp_g, w_out)


def kernel(x_prompt, x_sample, c_prompt, c_sample, ada_w, ada_b, norm_g, ffn_w1, ffn_w2,
           mix_w_in, conv_w, sg_norm_g, sg_ws, sg_bs, grp_norm_g, mix_w_out, final_g):
  n_prompt = c_prompt.shape[0]
  n_cond = n_prompt + c_sample.shape[0]
  c_all = jnp.concatenate(
      [c_prompt, c_sample, jnp.zeros((MOD_ROWS - n_cond, D_MODEL), F32)], axis=0)
  mod = _adaln(c_all, ada_w, ada_b).reshape(DEPTH, MOD_ROWS, N_MOD, D_MODEL)

  w1 = ffn_w1.astype(BF16)
  w2 = ffn_w2.astype(BF16)
  w_in = mix_w_in.astype(BF16)
  w_out = mix_w_out.astype(BF16)
  wcat = jnp.transpose(sg_ws, (0, 2, 1, 3)).reshape(
      DEPTH, CHUNK, N_HEADS_SG * CHUNK).astype(BF16)
  sbias = jnp.repeat(jnp.transpose(sg_bs, (0, 2, 1)), HEAD_DIM, axis=2)
  sgg = sg_norm_g.reshape(DEPTH, 1, D_SG)
  grp_g = grp_norm_g.reshape(DEPTH, 1, D_MIX)
  fg = final_g.reshape(1, D_MODEL)

  groups = [x_prompt.shape[:2], x_sample.shape[:2]]
  xs = [x_prompt.reshape(-1, D_MODEL), x_sample.reshape(-1, D_MODEL)]
  rows = [x.shape[0] for x in xs]
  all_rows = sum(rows)
  ffn_all = _Layout(groups, FFN_TILE)
  mix_all = _Layout(groups, MIX_TILE)
  ffn_one = [_Layout([grp], FFN_TILE, first_row=ffn_all.row0[g])
             for g, grp in enumerate(groups)]

  x = None
  for g in range(len(groups)):
    x = _ffn(xs[g], mod, norm_g, w1, w2, fg, layer=0, which=0, final=False, lay=ffn_one[g],
             out_rows=all_rows, out_tile0=ffn_all.tile0[g], dst=x)
  for l in range(DEPTH):
    if l > 0:
      x = _ffn(x, mod, norm_g, w1, w2, fg, layer=l, which=0, final=False, lay=ffn_all,
               out_rows=all_rows)
    x = _mixer(x, mod, norm_g, w_in, conv_w, sgg, wcat, sbias, grp_g, w_out, layer=l,
               lay=mix_all)
    if l < DEPTH - 1:
      x = _ffn(x, mod, norm_g, w1, w2, fg, layer=l, which=1, final=False, lay=ffn_all,
               out_rows=all_rows)
  outs = []
  for g, (bsz, seq) in enumerate(groups):
    y = _ffn(x, mod, norm_g, w1, w2, fg, layer=DEPTH - 1, which=1, final=True, lay=ffn_one[g],
             in_tile0=ffn_all.tile0[g], out_rows=rows[g])
    outs.append(y.reshape(bsz, seq, D_MODEL))
  return tuple(outs)
```

```python
import functools

import numpy as np
import jax
import jax.numpy as jnp
from jax import lax
from jax.experimental import pallas as pl
from jax.experimental.pallas import tpu as pltpu

D_MODEL = 1024
DEPTH = 2
HEAD_DIM = 64
N_HEADS_SG = 8
D_CONV = 512
D_SG = 512
D_MIX = D_CONV + D_SG
D_IN = 3 * D_CONV + 2 * D_SG
CHUNK = 128
D_FF = 2816
N_MOD = 9
EPS = 1e-6

BF16 = jnp.bfloat16
F32 = jnp.float32

MXU_COLS_V7X = 256
SUBLANES_F32 = 8
VMEM_LIMIT_BYTES = 56 * 1024 * 1024

FFN_TILE = 512
MIX_TILE = 1024
HALO_ROWS = 2 * SUBLANES_F32
FF_COLS = MXU_COLS_V7X
MOD_ROWS = 16
ADA_COLS = 1024
HEADS_PER_GROUP = 4
GROUP_COLS = HEADS_PER_GROUP * HEAD_DIM


def _resident(block_shape, index_map):
  return pl.BlockSpec(block_shape, index_map, pipeline_mode=pl.Buffered(1))


def _mod_norm(x, g, scale, shift):
  ms = jnp.mean(x * x, axis=-1, keepdims=True)
  y = (x * lax.rsqrt(ms + EPS)) * g
  return y * (1.0 + scale) + shift


def _rms(x, g):
  ms = jnp.mean(x * x, axis=-1, keepdims=True)
  return (x * lax.rsqrt(ms + EPS)) * g


class _Layout:
  def __init__(self, groups, tile, first_row=0):
    self.tiles_per_seq = [seq // tile for _, seq in groups]
    self.tile0, self.row0 = [], []
    tile0, row0 = 0, first_row
    for (bsz, _), tps in zip(groups, self.tiles_per_seq):
      self.tile0.append(tile0)
      self.row0.append(row0)
      tile0 += bsz * tps
      row0 += bsz
    self.n_tiles = tile0

  def _pick(self, tau, per_group):
    out = per_group(0)
    for g in range(1, len(self.tile0)):
      out = jnp.where(tau >= self.tile0[g], per_group(g), out)
    return out

  def row(self, tau):
    return self._pick(
        tau, lambda g: self.row0[g] + (tau - self.tile0[g]) // self.tiles_per_seq[g])

  def is_first(self, tau):
    return self._pick(tau, lambda g: (tau - self.tile0[g]) % self.tiles_per_seq[g] == 0)

  def is_last(self, tau):
    return self._pick(
        tau,
        lambda g: (tau - self.tile0[g]) % self.tiles_per_seq[g] == self.tiles_per_seq[g] - 1)


def _adaln_kernel(c_ref, w_ref, b_ref, o_ref):
  c = c_ref[...]
  sc = (c * jax.nn.sigmoid(c)).astype(BF16)
  w = w_ref[...].astype(BF16)
  o_ref[...] = jnp.dot(sc, w, preferred_element_type=F32) + b_ref[...]


def _adaln(c_all, ada_w, ada_b):
  n_out = N_MOD * D_MODEL
  return pl.pallas_call(
      _adaln_kernel,
      grid=(DEPTH, n_out // ADA_COLS),
      in_specs=[
          pl.BlockSpec((MOD_ROWS, D_MODEL), lambda l, j: (0, 0)),
          pl.BlockSpec((None, D_MODEL, ADA_COLS), lambda l, j: (l, 0, j)),
          pl.BlockSpec((None, 1, ADA_COLS), lambda l, j: (l, 0, j)),
      ],
      out_specs=pl.BlockSpec((None, MOD_ROWS, ADA_COLS), lambda l, j: (l, 0, j)),
      out_shape=jax.ShapeDtypeStruct((DEPTH, MOD_ROWS, n_out), F32),
      compiler_params=pltpu.CompilerParams(
          dimension_semantics=("arbitrary", "arbitrary"),
          vmem_limit_bytes=VMEM_LIMIT_BYTES),
      name="adaln_mod",
  )(c_all, ada_w, ada_b.reshape(DEPTH, 1, n_out))


def _ffn_kernel(k, final, n_cast, x_ref, xnext_ref, mod_ref, modnext_ref, ng_ref, w1_ref,
                w2_ref, fg_ref, *rest):
  h_scr, a_scr = rest[-2:]
  o_ref = rest[-3 - n_cast]
  for src, dst_ref in zip(rest[:n_cast], rest[-2 - n_cast:-2]):
    dst_ref[...] = src[...].astype(BF16)
  g = ng_ref[k:k + 1, :]

  def first_half(x_r, mod_r):
    x = x_r[...]
    inv = lax.rsqrt(jnp.mean(x * x, axis=-1, keepdims=True) + EPS)
    scale1 = 1.0 + mod_r[3 * k + 1:3 * k + 2, :]
    shift = mod_r[3 * k:3 * k + 1, :]
    for kb in range(D_MODEL // MXU_COLS_V7X):
      cs = slice(kb * MXU_COLS_V7X, (kb + 1) * MXU_COLS_V7X)
      h_scr[:, cs] = (((x_r[:, cs] * inv) * g[:, cs]) * scale1[:, cs]
                      + shift[:, cs]).astype(BF16)
    for c in range(D_FF // FF_COLS):
      lo = c * FF_COLS
      gt = jnp.dot(h_scr[...], w1_ref[:, lo:lo + FF_COLS], preferred_element_type=F32)
      up = jnp.dot(h_scr[...], w1_ref[:, D_FF + lo:D_FF + lo + FF_COLS],
                   preferred_element_type=F32)
      a_scr[:, lo:lo + FF_COLS] = ((gt * jax.nn.sigmoid(gt)) * up).astype(BF16)

  @pl.when(pl.program_id(0) == 0)
  def _():
    first_half(x_ref, mod_ref)

  out = jnp.dot(a_scr[...], w2_ref[...], preferred_element_type=F32)
  y = x_ref[...] + (0.5 * mod_ref[3 * k + 2:3 * k + 3, :]) * out
  if final:
    y = _rms(y, fg_ref[...])
  o_ref[...] = y
  first_half(xnext_ref, modnext_ref)


def _ffn(x2, mod, norm_g, w1, w2, final_g, *, layer, which, final, lay, in_tile0=0,
         out_rows, out_tile0=0, dst=None, w_index=None, cast=()):
  k = 2 * which
  t = FFN_TILE
  n = lay.n_tiles
  wl, wj = (layer, which) if w_index is None else w_index
  nxt = lambda s: jnp.minimum(s + 1, n - 1)
  in_specs = [
      pl.BlockSpec((t, D_MODEL), lambda s: (in_tile0 + s, 0)),
      pl.BlockSpec((t, D_MODEL), lambda s: (in_tile0 + nxt(s), 0)),
      pl.BlockSpec((None, None, N_MOD, D_MODEL), lambda s: (layer, lay.row(s), 0, 0)),
      pl.BlockSpec((None, None, N_MOD, D_MODEL), lambda s: (layer, lay.row(nxt(s)), 0, 0)),
      _resident((None, 3, D_MODEL), lambda s: (layer, 0, 0)),
      _resident((None, None, D_MODEL, 2 * D_FF), lambda s: (wl, wj, 0, 0)),
      _resident((None, None, D_FF, D_MODEL), lambda s: (wl, wj, 0, 0)),
      _resident((1, D_MODEL), lambda s: (0, 0)),
  ]
  operands = [x2, x2, mod, mod, norm_g, w1, w2, final_g]
  out_specs = [pl.BlockSpec((t, D_MODEL), lambda s: (out_tile0 + s, 0))]
  out_shape = [jax.ShapeDtypeStruct((out_rows, D_MODEL), F32)]
  for w in cast:
    grp, rows, cols = w.shape
    blk, per_grp = grp * rows // n, n // grp
    assert blk * n == grp * rows and per_grp * grp == n and blk % (2 * SUBLANES_F32) == 0
    spec = pl.BlockSpec((None, blk, cols), lambda s, p=per_grp: (s // p, s % p, 0))
    in_specs.append(spec)
    operands.append(w)
    out_specs.append(spec)
    out_shape.append(jax.ShapeDtypeStruct(w.shape, BF16))
  aliases = {}
  if dst is not None:
    in_specs.append(pl.BlockSpec(memory_space=pl.ANY))
    operands.append(dst)
    aliases = {len(operands) - 1: 0}
  res = pl.pallas_call(
      functools.partial(_ffn_kernel, k, final, len(cast)),
      grid=(n,),
      in_specs=in_specs,
      out_specs=out_specs,
      out_shape=out_shape,
      scratch_shapes=[pltpu.VMEM((t, D_MODEL), BF16), pltpu.VMEM((t, D_FF), BF16)],
      input_output_aliases=aliases,
      compiler_params=pltpu.CompilerParams(
          dimension_semantics=("arbitrary",),
          vmem_limit_bytes=VMEM_LIMIT_BYTES),
      name=f"ffn_l{layer}_{which}",
  )(*operands)
  return res if cast else res[0]


def _mixer_kernel(lay, x_ref, xp_ref, xn_ref, mod_ref, ng_ref, win_ref, cw_ref, sgg_ref,
                  wcat_ref, sbias_ref, gg_ref, wout_ref, o_ref, h_scr, y_scr):
  t = MIX_TILE
  tau = pl.program_id(0)
  shift = mod_ref[3:4, :]
  scale = mod_ref[4:5, :]
  gate = mod_ref[5:6, :]
  g = ng_ref[1:2, :]

  h_scr[0:t, :] = _mod_norm(x_ref[...], g, scale, shift).astype(BF16)
  x_halo = jnp.concatenate([xp_ref[...], xn_ref[...]], axis=0)
  h_scr[t:t + HALO_ROWS, :] = _mod_norm(x_halo, g, scale, shift).astype(BF16)

  zc = jnp.dot(h_scr[...], win_ref[:, D_CONV:2 * D_CONV], preferred_element_type=F32)
  zx = jnp.dot(h_scr[...], win_ref[:, 2 * D_CONV:3 * D_CONV], preferred_element_type=F32)
  p_ext = zc * zx
  p = p_ext[0:t, :]
  p_prev = jnp.where(lay.is_first(tau), 0.0,
                     p_ext[t + SUBLANES_F32 - 1:t + SUBLANES_F32, :])
  p_next = jnp.where(lay.is_last(tau), 0.0,
                     p_ext[t + SUBLANES_F32:t + SUBLANES_F32 + 1, :])
  row = lax.broadcasted_iota(jnp.int32, (t, 1), 0)
  p_m1 = jnp.where(row == 0, p_prev, pltpu.roll(p, 1, axis=0))
  p_p1 = jnp.where(row == t - 1, p_next, pltpu.roll(p, t - 1, axis=0))
  conv = p_m1 * cw_ref[0:1, :] + p * cw_ref[1:2, :] + p_p1 * cw_ref[2:3, :]
  b_gate = jnp.dot(h_scr[0:t, :], win_ref[:, 0:D_CONV], preferred_element_type=F32)
  y_a = b_gate * conv
  y_scr[:, 0:D_CONV] = _rms(y_a, gg_ref[:, 0:D_CONV]).astype(BF16)

  sqrt_half = np.float32(np.sqrt(0.5))
  zu = jnp.dot(h_scr[0:t, :], win_ref[:, 3 * D_CONV:3 * D_CONV + D_SG],
               preferred_element_type=F32)
  zv = jnp.dot(h_scr[0:t, :], win_ref[:, 3 * D_CONV + D_SG:D_IN],
               preferred_element_type=F32)
  u = 0.5 * zu * (1.0 + lax.erf(zu * sqrt_half))
  v = 0.5 * zv * (1.0 + lax.erf(zv * sqrt_half))
  mu = jnp.mean(v, axis=-1, keepdims=True)
  dv = v - mu
  v = ((dv * lax.rsqrt(jnp.mean(dv * dv, axis=-1, keepdims=True) + EPS))
       * sgg_ref[...])
  lane_head = lax.broadcasted_iota(jnp.int32, (CHUNK, GROUP_COLS), 1) // HEAD_DIM
  sv_chunks = []
  for n in range(t // CHUNK):
    groups = []
    for gi in range(N_HEADS_SG // HEADS_PER_GROUP):
      vc = v[n * CHUNK:(n + 1) * CHUNK, gi * GROUP_COLS:(gi + 1) * GROUP_COLS]
      v_blk = jnp.concatenate(
          [jnp.where(lane_head == j, vc, 0.0).astype(BF16)
           for j in range(HEADS_PER_GROUP)], axis=0)
      w_g = wcat_ref[:, gi * HEADS_PER_GROUP * CHUNK:(gi + 1) * HEADS_PER_GROUP * CHUNK]
      groups.append(jnp.dot(w_g, v_blk, preferred_element_type=F32))
    sv_chunks.append(jnp.concatenate(groups, axis=1) + sbias_ref[...])
  sv = jnp.concatenate(sv_chunks, axis=0)
  y_b = u * sv
  y_scr[:, D_CONV:D_MIX] = _rms(y_b, gg_ref[:, D_CONV:D_MIX]).astype(BF16)

  out = (jnp.dot(y_scr[:, 0:D_CONV], wout_ref[0:D_CONV, :], preferred_element_type=F32)
         + jnp.dot(y_scr[:, D_CONV:D_MIX], wout_ref[D_CONV:D_MIX, :],
                   preferred_element_type=F32))
  o_ref[...] = x_ref[...] + gate * out


def _mixer(x2, mod, norm_g, w_in, conv_w, sg_norm_g, wcat, sbias, grp_g, w_out, *,
           layer, lay):
  t = MIX_TILE
  r = t // SUBLANES_F32
  n_blk = x2.shape[0] // SUBLANES_F32
  return pl.pallas_call(
      functools.partial(_mixer_kernel, lay),
      grid=(lay.n_tiles,),
      in_specs=[
          pl.BlockSpec((t, D_MODEL), lambda s: (s, 0)),
          pl.BlockSpec((SUBLANES_F32, D_MODEL), lambda s: (jnp.maximum(s * r - 1, 0), 0)),
          pl.BlockSpec((SUBLANES_F32, D_MODEL),
                       lambda s: (jnp.minimum((s + 1) * r, n_blk - 1), 0)),
          pl.BlockSpec((None, None, N_MOD, D_MODEL), lambda s: (layer, lay.row(s), 0, 0)),
          _resident((None, 3, D_MODEL), lambda s: (layer, 0, 0)),
          _resident((None, D_MODEL, D_IN), lambda s: (layer, 0, 0)),
          _resident((None, 3, D_CONV), lambda s: (layer, 0, 0)),
          _resident((None, 1, D_SG), lambda s: (layer, 0, 0)),
          _resident((None, CHUNK, N_HEADS_SG * CHUNK), lambda s: (layer, 0, 0)),
          _resident((None, CHUNK, D_SG), lambda s: (layer, 0, 0)),
          _resident((None, 1, D_MIX), lambda s: (layer, 0, 0)),
          _resident((None, D_MIX, D_MODEL), lambda s: (layer, 0, 0)),
      ],
      out_specs=pl.BlockSpec((t, D_MODEL), lambda s: (s, 0)),
      out_shape=jax.ShapeDtypeStruct(x2.shape, F32),
      scratch_shapes=[pltpu.VMEM((t + HALO_ROWS, D_MODEL), BF16),
                      pltpu.VMEM((t, D_MIX), BF16)],
      compiler_params=pltpu.CompilerParams(
          dimension_semantics=("arbitrary",),
          vmem_limit_bytes=VMEM_LIMIT_BYTES),
      name=f"mixer_l{layer}",
  )(x2, x2, x2, mod, norm_g, w_in, conv_w, sg_norm_g, wcat, sbias, grp_g, w_out)


def kernel(x_prompt, x_sample, c_prompt, c_sample, ada_w, ada_b, norm_g, ffn_w1, ffn_w2,
           mix_w_in, conv_w, sg_norm_g, sg_ws, sg_bs, grp_norm_g, mix_w_out, final_g):
  n_prompt = c_prompt.shape[0]
  n_cond = n_prompt + c_sample.shape[0]
  c_all = jnp.concatenate(
      [c_prompt, c_sample, jnp.zeros((MOD_ROWS - n_cond, D_MODEL), F32)], axis=0)
  mod = _adaln(c_all, ada_w, ada_b).reshape(DEPTH, MOD_ROWS, N_MOD, D_MODEL)

  w1_first = ffn_w1[:1, :1].astype(BF16)
  w2_first = ffn_w2[:1, :1].astype(BF16)
  wcat = jnp.transpose(sg_ws, (0, 2, 1, 3)).reshape(
      DEPTH, CHUNK, N_HEADS_SG * CHUNK).astype(BF16)
  sbias = jnp.repeat(jnp.transpose(sg_bs, (0, 2, 1)), HEAD_DIM, axis=2)
  sgg = sg_norm_g.reshape(DEPTH, 1, D_SG)
  grp_g = grp_norm_g.reshape(DEPTH, 1, D_MIX)
  fg = final_g.reshape(1, D_MODEL)

  groups = [x_prompt.shape[:2], x_sample.shape[:2]]
  xs = [x_prompt.reshape(-1, D_MODEL), x_sample.reshape(-1, D_MODEL)]
  rows = [x.shape[0] for x in xs]
  all_rows = sum(rows)
  ffn_all = _Layout(groups, FFN_TILE)
  mix_all = _Layout(groups, MIX_TILE)
  ffn_one = [_Layout([grp], FFN_TILE, first_row=ffn_all.row0[g])
             for g, grp in enumerate(groups)]

  to_cast = (ffn_w1.reshape(-1, D_MODEL, 2 * D_FF), ffn_w2.reshape(-1, D_FF, D_MODEL),
             mix_w_in, mix_w_out)
  x, w1, w2, w_in, w_out = _ffn(
      xs[0], mod, norm_g, w1_first, w2_first, fg, layer=0, which=0, final=False,
      lay=ffn_one[0], out_rows=all_rows, out_tile0=ffn_all.tile0[0], w_index=(0, 0),
      cast=to_cast)
  w1 = w1.reshape(ffn_w1.shape)
  w2 = w2.reshape(ffn_w2.shape)
  for g in range(1, len(groups)):
    x = _ffn(xs[g], mod, norm_g, w1_first, w2_first, fg, layer=0, which=0, final=False,
             lay=ffn_one[g], out_rows=all_rows, out_tile0=ffn_all.tile0[g], dst=x,
             w_index=(0, 0))
  for l in range(DEPTH):
    if l > 0:
      x = _ffn(x, mod, norm_g, w1, w2, fg, layer=l, which=0, final=False, lay=ffn_all,
               out_rows=all_rows)
    x = _mixer(x, mod, norm_g, w_in, conv_w, sgg, wcat, sbias, grp_g, w_out, layer=l,
               lay=mix_all)
    if l < DEPTH - 1:
      x = _ffn(x, mod, norm_g, w1, w2, fg, layer=l, which=1, final=False, lay=ffn_all,
               out_rows=all_rows)
  outs = []
  for g, (bsz, seq) in enumerate(groups):
    y = _ffn(x, mod, norm_g, w1, w2, fg, layer=DEPTH - 1, which=1, final=True, lay=ffn_one[g],
             in_tile0=ffn_all.tile0[g], out_rows=rows[g])
    outs.append(y.reshape(bsz, seq, D_MODEL))
  return tuple(outs)
```
